```python
import math
import jax, jax.numpy as jnp
from jax import lax
import numpy as np

D_MODEL = 2048
BATCH = 4
SEQ = 2048
DEPTH = 4
DEC_BATCH = 8
DEC_SEQ = 8
PAST_LEN = 16384
PAGE_SIZE = 128

H_R = 8
DR = 128
DVR = 128
W_R = H_R * DR
RET_CHUNK = 128
ROPE_BASE = 10000.0
H_D = 8
DQK = 64
DVD = 2 * DQK
W_DQ = H_D * 2 * DQK
W_D = H_D * DVD
Q_BLOCK = 128
SPLITS = [W_R, 2 * W_R, 3 * W_R, 4 * W_R,
          4 * W_R + W_DQ, 4 * W_R + 2 * W_DQ, 4 * W_R + 2 * W_DQ + W_D,
          4 * W_R + 2 * W_DQ + W_D + D_MODEL]
N_IN = 4 * W_R + 2 * W_DQ + W_D + 2 * D_MODEL
FF_DENSE = 5504
N_EXPERTS = 8
TOP_K = 2
FF_EXPERT = 7168
EPS = 1e-6

kernel_name = "hybrid_retnet_diffattn_adaln_moe_step"


def rms_norm(x, g):
    xf = x.astype(jnp.float32)
    return xf * lax.rsqrt(jnp.mean(xf * xf, -1, keepdims=True) + EPS) * g


def head_layernorm(x, g):
    xf = x.astype(jnp.float32)
    xc = xf - jnp.mean(xf, -1, keepdims=True)
    return xc * lax.rsqrt(jnp.mean(xc * xc, -1, keepdims=True) + EPS) * g


def adaln_params(c, w, b):
    m = jax.nn.silu(c) @ w + b
    return jnp.split(m[:, None, :], 6, axis=-1)


def rope(x, pos):
    half = x.shape[-1] // 2
    inv = 1.0 / (ROPE_BASE ** (jnp.arange(half, dtype=jnp.float32) / half))
    ang = pos.astype(jnp.float32)[:, None] * inv[None, :]
    cos = jnp.cos(ang)[None, :, None, :]
    sin = jnp.sin(ang)[None, :, None, :]
    x1, x2 = x[..., :half], x[..., half:]
    return jnp.concatenate([x1 * cos - x2 * sin, x1 * sin + x2 * cos], axis=-1)


def retention_log_decay():
    return jnp.log1p(-jnp.exp2(-5.0 - jnp.arange(H_R, dtype=jnp.float32)))


def retention_chunk(q, k, v, R, log_g):
    C = q.shape[1]
    idx = jnp.arange(C, dtype=jnp.float32)
    diff = idx[:, None] - idx[None, :]
    causal = diff >= 0
    dmat = jnp.where(causal[None], jnp.exp(log_g[:, None, None] * jnp.where(causal, diff, 0.0)[None]), 0.0)
    s = jnp.einsum('bqhd,bkhd->bhqk', q, k) * dmat[None]
    o_in = jnp.einsum('bhqk,bkhe->bqhe', s, v)
    xi = jnp.exp(log_g[None, :] * (idx[:, None] + 1.0))
    o_cross = jnp.einsum('bqhd,bhde->bqhe', q * xi[None, :, :, None], R)
    zeta = jnp.exp(log_g[None, :] * (C - 1.0 - idx)[:, None])
    R_new = jnp.exp(log_g * C)[None, :, None, None] * R + jnp.einsum(
        'bkhd,bkhe->bhde', k * zeta[None, :, :, None], v)
    return o_in + o_cross, R_new


def retention_scan(q, k, v, log_g):
    B, S, H, d = q.shape
    nc = S // RET_CHUNK

    def to_chunks(t):
        return t.reshape(B, nc, RET_CHUNK, H, t.shape[-1]).transpose(1, 0, 2, 3, 4)

    R0 = jnp.zeros((B, H, d, v.shape[-1]), jnp.float32)

    def body(R, inp):
        qc, kc, vc = inp
        o, R = retention_chunk(qc, kc, vc, R, log_g)
        return R, o

    R, o = lax.scan(body, R0, (to_chunks(q), to_chunks(k), to_chunks(v)))
    return o.transpose(1, 0, 2, 3, 4).reshape(B, S, H, -1), R


def diff_attention(q, k, v, qpos, kpos, lam):
    B, T = q.shape[:2]
    nb = T // Q_BLOCK if (T % Q_BLOCK == 0 and T > Q_BLOCK) else 1
    blk = T // nb
    scale = DQK ** -0.5
    k = k.astype(jnp.float32)
    v = v.astype(jnp.float32)
    qb = q.astype(jnp.float32).reshape(B, nb, blk, H_D, 2, DQK).transpose(1, 0, 2, 3, 4, 5)
    qposb = qpos.reshape(nb, blk)
    neg = jnp.finfo(jnp.float32).min

    def block(args):
        qi, qp = args
        s = jnp.einsum('bqhmd,bkhmd->bhmqk', qi, k) * scale
        mask = kpos[None, :] <= qp[:, None]
        p = jax.nn.softmax(jnp.where(mask, s, neg), axis=-1)
        a = p[:, :, 0] - lam * p[:, :, 1]
        return jnp.einsum('bhqk,bkhe->bqhe', a, v)

    o = lax.map(block, (qb, qposb))
    return o.transpose(1, 0, 2, 3, 4).reshape(B, T, H_D, DVD)


def mixer_inputs(h, w_in_l, pos, g_qn_l, g_kn_l):
    B, S, _ = h.shape
    z = h @ w_in_l
    rq, rk, rv, rg, dq, dk, dv, ga, gb = jnp.split(z, SPLITS, axis=-1)
    rq = rope(rq.reshape(B, S, H_R, DR).astype(jnp.float32), pos)
    rk = rope(rk.reshape(B, S, H_R, DR).astype(jnp.float32), pos) * (DR ** -0.5)
    rv = rv.reshape(B, S, H_R, DVR).astype(jnp.float32)
    dq = rms_norm(dq.reshape(B, S, H_D, 2, DQK), g_qn_l)
    dk = rms_norm(dk.reshape(B, S, H_D, 2, DQK), g_kn_l)
    dv = dv.reshape(B, S, H_D, DVD)
    return rq, rk, rv, rg, dq, dk, dv, ga, gb


def dense_ffn(h, w1, w3, w2):
    return (jax.nn.silu(h @ w1) * (h @ w3)) @ w2


def moe_ffn(h, w_router, w1, w3, w2):
    logits = (h @ w_router).astype(jnp.float32)
    top_v, top_i = lax.top_k(logits, TOP_K)
    wts = jax.nn.softmax(top_v, axis=-1)
    gate = jnp.sum(jax.nn.one_hot(top_i, N_EXPERTS, dtype=jnp.float32) * wts[..., None], axis=-2)
    out = jnp.zeros(h.shape, jnp.float32)
    for e in range(N_EXPERTS):
        he = jax.nn.silu(h @ w1[e]) * (h @ w3[e])
        out = out + gate[..., e:e + 1] * (he @ w2[e])
    return out.astype(h.dtype)


def layer_step(x, c, pos, kv_past, R_past, w_ada_l, b_ada_l, g1, g2, w_in_l, g_qn_l, g_kn_l,
               lam, lam_init, g_ret_l, g_dn_l, w_pa_l, w_pb_l, w_o_l, ffn_fn):
    B, S, _ = x.shape
    sh1, sc1, gt1, sh2, sc2, gt2 = adaln_params(c, w_ada_l, b_ada_l)
    h = (rms_norm(x, g1) * (1.0 + sc1) + sh1).astype(x.dtype)
    rq, rk, rv, rg, dq, dk, dv, ga, gb = mixer_inputs(h, w_in_l, pos, g_qn_l, g_kn_l)
    log_g = retention_log_decay()
    if R_past is None:
        o_ret, R_new = retention_scan(rq, rk, rv, log_g)
        k_all, v_all, kpos = dk, dv, pos
    else:
        o_ret, R_new = retention_chunk(rq, rk, rv, R_past.astype(jnp.float32), log_g)
        k_past, v_past = kv_past
        k_all = jnp.concatenate([k_past.astype(jnp.float32), dk], axis=1)
        v_all = jnp.concatenate([v_past.astype(jnp.float32), dv.astype(jnp.float32)], axis=1)
        kpos = jnp.arange(k_all.shape[1])
    o_diff = diff_attention(dq, k_all, v_all, pos, kpos, lam)
    yr = ((head_layernorm(o_ret, g_ret_l).reshape(B, S, W_R) * jax.nn.silu(rg.astype(jnp.float32)))
          .astype(x.dtype) @ w_pa_l)
    yd = ((rms_norm(o_diff, g_dn_l) * (1.0 - lam_init)).reshape(B, S, W_D).astype(x.dtype) @ w_pb_l)
    mixed = (jax.nn.sigmoid(ga) * yr + jax.nn.sigmoid(gb) * yd) @ w_o_l
    x = x + (gt1 * mixed).astype(x.dtype)
    h2 = (rms_norm(x, g2) * (1.0 + sc2) + sh2).astype(x.dtype)
    x = x + (gt2 * ffn_fn(h2)).astype(x.dtype)
    return x, dk.reshape(B, S, H_D, 2 * DQK), dv, R_new


def setup_inputs(seed: int = 0) -> dict:
    key = jax.random.key(seed)
    ks = jax.random.split(key, 40)
    f32 = jnp.float32
    n_pages = PAST_LEN // PAGE_SIZE
    n_used = DEC_BATCH * n_pages
    n_pool = n_used + n_used // 4
    nd = (DEPTH + 1) // 2
    nm = DEPTH // 2

    def nrm(k, shape, scale):
        return jax.random.normal(k, shape, f32) * scale

    def gain(k, shape):
        return 1.0 + 0.05 * jax.random.normal(k, shape, f32)

    page_table = jax.random.permutation(ks[7], n_pool)[:n_used].reshape(DEC_BATCH, n_pages).astype(jnp.int32)
    return {
        "x_prompt": nrm(ks[0], (BATCH, SEQ, D_MODEL), 1.0),
        "x_sample": nrm(ks[1], (DEC_BATCH, DEC_SEQ, D_MODEL), 1.0),
        "cache_k": nrm(ks[2], (DEPTH, n_pool, PAGE_SIZE, H_D, 2 * DQK), 1.0),
        "cache_v": nrm(ks[3], (DEPTH, n_pool, PAGE_SIZE, H_D, DVD), 1.0),
        "state_ret": nrm(ks[4], (DEPTH, DEC_BATCH, H_R, DR, DVR), 0.5),
        "page_table": page_table,
        "c_prompt": nrm(ks[5], (BATCH, D_MODEL), 1.0),
        "c_sample": nrm(ks[6], (DEC_BATCH, D_MODEL), 1.0),
        "w_ada": nrm(ks[8], (DEPTH, D_MODEL, 6 * D_MODEL), 0.5 * D_MODEL ** -0.5),
        "b_ada": nrm(ks[9], (DEPTH, 6 * D_MODEL), 0.02),
        "g_norm1": gain(ks[10], (DEPTH, D_MODEL)),
        "g_norm2": gain(ks[11], (DEPTH, D_MODEL)),
        "w_in": nrm(ks[12], (DEPTH, D_MODEL, N_IN), D_MODEL ** -0.5),
        "g_qn": gain(ks[13], (DEPTH, DQK)),
        "g_kn": gain(ks[14], (DEPTH, DQK)),
        "lam_q1": nrm(ks[15], (DEPTH, DQK), 0.1),
        "lam_k1": nrm(ks[16], (DEPTH, DQK), 0.1),
        "lam_q2": nrm(ks[17], (DEPTH, DQK), 0.1),
        "lam_k2": nrm(ks[18], (DEPTH, DQK), 0.1),
        "g_ret": gain(ks[19], (DEPTH, H_R, DVR)),
        "g_dn": gain(ks[20], (DEPTH, H_D, DVD)),
        "w_pa": nrm(ks[21], (DEPTH, W_R, D_MODEL), W_R ** -0.5),
        "w_pb": nrm(ks[22], (DEPTH, W_D, D_MODEL), W_D ** -0.5),
        "w_o": nrm(ks[23], (DEPTH, D_MODEL, D_MODEL), D_MODEL ** -0.5),
        "w_ff1": nrm(ks[24], (nd, D_MODEL, FF_DENSE), D_MODEL ** -0.5),
        "w_ff3": nrm(ks[25], (nd, D_MODEL, FF_DENSE), D_MODEL ** -0.5),
        "w_ff2": nrm(ks[26], (nd, FF_DENSE, D_MODEL), FF_DENSE ** -0.5),
        "w_router": nrm(ks[27], (nm, D_MODEL, N_EXPERTS), D_MODEL ** -0.5),
        "w_e1": nrm(ks[28], (nm, N_EXPERTS, D_MODEL, FF_EXPERT), D_MODEL ** -0.5),
        "w_e3": nrm(ks[29], (nm, N_EXPERTS, D_MODEL, FF_EXPERT), D_MODEL ** -0.5),
        "w_e2": nrm(ks[30], (nm, N_EXPERTS, FF_EXPERT, D_MODEL), FF_EXPERT ** -0.5),
    }


def reference(x_prompt, x_sample, cache_k, cache_v, state_ret, page_table, c_prompt, c_sample,
              w_ada, b_ada, g_norm1, g_norm2, w_in, g_qn, g_kn, lam_q1, lam_k1, lam_q2, lam_k2,
              g_ret, g_dn, w_pa, w_pb, w_o, w_ff1, w_ff3, w_ff2, w_router, w_e1, w_e3, w_e2):
    Bd, n_pages = page_table.shape
    ps = cache_k.shape[2]
    past = n_pages * ps
    pos_p = jnp.arange(x_prompt.shape[1])
    pos_s = past + jnp.arange(x_sample.shape[1])
    xp, xs = x_prompt, x_sample
    kp, vp, rp, ksl, vsl, rsl = [], [], [], [], [], []
    for l in range(DEPTH):
        lam_init = 0.8 - 0.6 * math.exp(-0.3 * l)
        lam = (jnp.exp(jnp.sum(lam_q1[l].astype(jnp.float32) * lam_k1[l].astype(jnp.float32)))
               - jnp.exp(jnp.sum(lam_q2[l].astype(jnp.float32) * lam_k2[l].astype(jnp.float32))) + lam_init)
        i = l // 2
        if l % 2 == 0:
            ffn_fn = lambda h, i=i: dense_ffn(h, w_ff1[i], w_ff3[i], w_ff2[i])
        else:
            ffn_fn = lambda h, i=i: moe_ffn(h, w_router[i], w_e1[i], w_e3[i], w_e2[i])
        common = (w_ada[l], b_ada[l], g_norm1[l], g_norm2[l], w_in[l], g_qn[l], g_kn[l], lam, lam_init,
                  g_ret[l], g_dn[l], w_pa[l], w_pb[l], w_o[l], ffn_fn)
        xp, k_new, v_new, R_new = layer_step(xp, c_prompt, pos_p, None, None, *common)
        kp.append(k_new.astype(cache_k.dtype)); vp.append(v_new.astype(cache_v.dtype)); rp.append(R_new.astype(state_ret.dtype))
        k_past = cache_k[l][page_table].reshape(Bd, past, H_D, 2, DQK)
        v_past = cache_v[l][page_table].reshape(Bd, past, H_D, DVD)
        xs, k_new, v_new, R_new = layer_step(xs, c_sample, pos_s, (k_past, v_past), state_ret[l], *common)
        ksl.append(k_new.astype(cache_k.dtype)); vsl.append(v_new.astype(cache_v.dtype)); rsl.append(R_new.astype(state_ret.dtype))
    return (xp, xs, jnp.stack(kp), jnp.stack(vp), jnp.stack(rp), jnp.stack(ksl), jnp.stack(vsl), jnp.stack(rsl))
```

```python
import functools
import math

import jax
import jax.numpy as jnp
from jax import lax
from jax.experimental import pallas as pl
from jax.experimental.pallas import tpu as pltpu

F32 = jnp.float32
BF16 = jnp.bfloat16

D_MODEL = 2048
DEPTH = 4
PAGE_SIZE = 128
H_R = 8
DR = 128
W_R = H_R * DR
RET_CHUNK = 128
ROPE_BASE = 10000.0
H_D = 8
DQK = 64
DVD = 2 * DQK
W_DQ = H_D * 2 * DQK
W_D = H_D * DVD
N_IN = 4 * W_R + 2 * W_DQ + W_D + 2 * D_MODEL
N_EXPERTS = 8
TOP_K = 2
EPS = 1e-6
NEG = -1e30

LANES = 128
VMEM_CAP_BYTES = 60000 * 1024

COL_RQ, COL_RK, COL_RV, COL_RG, COL_DQ, COL_DK, COL_DV, COL_GA, COL_GB = 0, 1, 2, 3, 4, 5, 6, 7, 9


def _cparams(n_axes, vmem_bytes):
    return pltpu.CompilerParams(
        dimension_semantics=("arbitrary",) * n_axes,
        vmem_limit_bytes=min(int(vmem_bytes), VMEM_CAP_BYTES))


def _mm(a, b):
    return jnp.dot(a, b, preferred_element_type=F32)


def _mm_nt(a, b):
    return lax.dot_general(a, b, (((1,), (1,)), ((), ())), preferred_element_type=F32)


def _mm_tn(a, b):
    return lax.dot_general(a, b, (((0,), (0,)), ((), ())), preferred_element_type=F32)


def _silu(x):
    return x * jax.nn.sigmoid(x)


def _adaln_kernel(c_ref, w_ref, b_ref, o_ref):
    s = _silu(c_ref[...]).astype(BF16)
    o_ref[...] = _mm(s, w_ref[...].astype(BF16)) + b_ref[...]


def _adaln(c_all, w_ada, b_ada):
    rows = c_all.shape[0]
    tn = 1536
    n_out = 6 * D_MODEL
    return pl.pallas_call(
        _adaln_kernel,
        grid=(DEPTH, n_out // tn),
        in_specs=[
            pl.BlockSpec((rows, D_MODEL), lambda l, n: (0, 0)),
            pl.BlockSpec((None, D_MODEL, tn), lambda l, n: (l, 0, n)),
            pl.BlockSpec((None, 1, tn), lambda l, n: (l, 0, n)),
        ],
        out_specs=pl.BlockSpec((None, rows, tn), lambda l, n: (l, 0, n)),
        out_shape=jax.ShapeDtypeStruct((DEPTH, rows, n_out), F32),
        compiler_params=_cparams(2, 2 * D_MODEL * tn * 4 + D_MODEL * tn * 2 + (8 << 20)),
        name="adaln",
    )(c_all, w_ada, b_ada.reshape(DEPTH, 1, n_out))


class _Mods:
    def __init__(self, mod_l, group, rows_per_batch=None):
        self.group = group
        if group == "prompt":
            self.arr = mod_l.reshape(mod_l.shape[0], 6, 1, D_MODEL)
            self.rows_per_batch = rows_per_batch
        else:
            n_rep = rows_per_batch
            self.arr = jnp.transpose(jnp.repeat(mod_l, n_rep, axis=0), (1, 0, 2))

    def spec(self, j, tm, n_axes, col_block=None, tn=D_MODEL):
        if self.group == "prompt":
            per = self.rows_per_batch // tm

            def imap(*g):
                return (g[0] // per, j, 0, 0 if col_block is None else col_block(*g))
            return pl.BlockSpec((None, None, 1, tn), imap)

        def imap_s(*g):
            return (j, 0, 0 if col_block is None else col_block(*g))
        return pl.BlockSpec((None, self.arr.shape[1], tn), imap_s)


def _inproj_kernel(x_ref, g_ref, sc_ref, sh_ref, w_ref, gq_ref, gk_ref, bd_ref, o_ref, h_scr, *, tn):
    n = pl.program_id(1)

    @pl.when(n == 0)
    def _():
        x = x_ref[...]
        ms = jnp.mean(x * x, axis=-1, keepdims=True)
        h = x * lax.rsqrt(ms + EPS) * g_ref[...]
        h_scr[...] = (h * (1.0 + sc_ref[...]) + sh_ref[...]).astype(BF16)

    z = _mm(h_scr[...], w_ref[...].astype(BF16))
    per = 1024 // tn
    is_q = (n >= COL_DQ * per) & (n < (COL_DQ + 1) * per)
    is_k = (n >= COL_DK * per) & (n < (COL_DK + 1) * per)
    normed = is_q | is_k

    @pl.when(normed)
    def _():
        ms = _mm((z * z).astype(BF16), bd_ref[...])
        g = jnp.where(is_q, gq_ref[...], gk_ref[...])
        o_ref[...] = z * lax.rsqrt(ms + EPS) * g

    @pl.when(jnp.logical_not(normed))
    def _():
        o_ref[...] = z


def _inproj(x, mods, l, g_norm1, w_in, g_qn, g_kn, tm):
    T = x.shape[0]
    tn = 512
    grp = jnp.arange(tn) // DQK
    bd = jnp.where(grp[:, None] == grp[None, :], 1.0 / DQK, 0.0).astype(BF16)
    gq = jnp.tile(g_qn[l], tn // DQK)[None]
    gk = jnp.tile(g_kn[l], tn // DQK)[None]
    vmem = (2 * tm * D_MODEL * 4 + 2 * D_MODEL * tn * 4 + 2 * tm * tn * 4 + tm * D_MODEL * 2
            + D_MODEL * tn * 2 + 3 * tm * tn * 4 + 2 * tn * tn * 2 + (6 << 20))
    return pl.pallas_call(
        functools.partial(_inproj_kernel, tn=tn),
        grid=(T // tm, N_IN // tn),
        in_specs=[
            pl.BlockSpec((tm, D_MODEL), lambda m, n: (m, 0)),
            pl.BlockSpec((1, D_MODEL), lambda m, n: (0, 0)),
            mods.spec(1, tm, 2),
            mods.spec(0, tm, 2),
            pl.BlockSpec((None, D_MODEL, tn), lambda m, n: (l, 0, n)),
            pl.BlockSpec((1, tn), lambda m, n: (0, 0)),
            pl.BlockSpec((1, tn), lambda m, n: (0, 0)),
            pl.BlockSpec((tn, tn), lambda m, n: (0, 0)),
        ],
        out_specs=pl.BlockSpec((tm, tn), lambda m, n: (m, n)),
        out_shape=jax.ShapeDtypeStruct((T, N_IN), F32),
        scratch_shapes=[pltpu.VMEM((tm, D_MODEL), BF16)],
        compiler_params=_cparams(2, vmem),
        name="inproj",
    )(x, g_norm1[l][None], mods.arr, mods.arr, w_in, gq, gk, bd)


def _retention_kernel(zq, zk, zv, zg, cos_ref, sin_ref, dmat, xi, zeta, gc, gr, r0, yr, rout, r_scr, *, nc):
    c = pl.program_id(1)

    @pl.when(c == 0)
    def _():
        r_scr[...] = r0[...]

    cs = cos_ref[...]
    sn = sin_ref[...]
    for h in range(H_R):
        sl = slice(h * DR, (h + 1) * DR)
        q = zq[:, sl]
        k = zk[:, sl]
        q = q * cs + pltpu.roll(q, DR // 2, 1) * sn
        k = (k * cs + pltpu.roll(k, DR // 2, 1) * sn) * (DR ** -0.5)
        vb = zv[:, sl].astype(BF16)
        s = _mm_nt(q.astype(BF16), k.astype(BF16)) * dmat[h]
        r_h = r_scr[h]
        o = _mm(s.astype(BF16), vb) + _mm((q * xi[h]).astype(BF16), r_h.astype(BF16))
        r_scr[h] = gc[h] * r_h + _mm_tn((k * zeta[h]).astype(BF16), vb)
        oc = o - jnp.mean(o, axis=-1, keepdims=True)
        y = oc * lax.rsqrt(jnp.mean(oc * oc, axis=-1, keepdims=True) + EPS) * gr[h]
        yr[:, sl] = (y * _silu(zg[:, sl])).astype(yr.dtype)

    @pl.when(c == nc - 1)
    def _():
        rout[...] = r_scr[...]


def _retention_consts(C):
    log_g = jnp.log1p(-jnp.exp2(-5.0 - jnp.arange(H_R, dtype=F32)))
    idx = jnp.arange(C, dtype=F32)
    diff = idx[:, None] - idx[None, :]
    causal = diff >= 0
    dmat = jnp.where(causal[None], jnp.exp(log_g[:, None, None] * jnp.where(causal, diff, 0.0)[None]), 0.0)
    xi = jnp.exp(log_g[:, None] * (idx[None, :] + 1.0))
    zeta = jnp.exp(log_g[:, None] * (C - 1.0 - idx)[None, :])
    gc = jnp.exp(log_g * C)
    bc = lambda t: jnp.broadcast_to(t[:, :, None], (H_R, C, DR))
    return dmat, bc(xi), bc(zeta), jnp.broadcast_to(gc[:, None, None], (H_R, 1, DR))


def _rope_tables(pos):
    half = DR // 2
    inv = 1.0 / (ROPE_BASE ** (jnp.arange(half, dtype=F32) / half))
    ang = pos.astype(F32)[:, None] * inv[None, :]
    cos, sin = jnp.cos(ang), jnp.sin(ang)
    return jnp.concatenate([cos, cos], -1), jnp.concatenate([-sin, sin], -1)


def _retention(z, pos, r0, g_ret_l, n_batch, C, out_dtype):
    T = z.shape[0]
    nc = T // (n_batch * C)
    cos, sin = _rope_tables(pos)
    dmat, xi, zeta, gc = _retention_consts(C)
    zspec = lambda col: pl.BlockSpec((C, W_R), lambda b, c: (b * nc + c, col))
    full = lambda shp: pl.BlockSpec(shp, lambda b, c: (0,) * len(shp))
    state_spec = pl.BlockSpec((None, H_R, DR, DR), lambda b, c: (b, 0, 0, 0))
    vmem = 2 * 5 * C * W_R * 4 + 2 * (H_R * C * C + 2 * H_R * C * DR) * 4 + 5 * H_R * DR * DR * 4 + (8 << 20)
    return pl.pallas_call(
        functools.partial(_retention_kernel, nc=nc),
        grid=(n_batch, nc),
        in_specs=[zspec(COL_RQ), zspec(COL_RK), zspec(COL_RV), zspec(COL_RG),
                  pl.BlockSpec((C, DR), lambda b, c: (c, 0)),
                  pl.BlockSpec((C, DR), lambda b, c: (c, 0)),
                  full((H_R, C, C)), full((H_R, C, DR)), full((H_R, C, DR)), full((H_R, 1, DR)),
                  full((H_R, 1, DR)), state_spec],
        out_specs=[pl.BlockSpec((C, W_R), lambda b, c: (b * nc + c, 0)), state_spec],
        out_shape=[jax.ShapeDtypeStruct((T, W_R), out_dtype),
                   jax.ShapeDtypeStruct((n_batch, H_R, DR, DR), F32)],
        scratch_shapes=[pltpu.VMEM((H_R, DR, DR), F32)],
        compiler_params=_cparams(2, vmem),
        name="retention",
    )(z, z, z, z, cos, sin, dmat, xi, zeta, gc, g_ret_l[:, None, :], r0)


def _stack_queries(q, qs_ref, n):
    lane = lax.broadcasted_iota(jnp.int32, q.shape, 1)
    q = q * (DQK ** -0.5)
    qs_ref[0:n] = jnp.where(lane < DQK, q, 0.0).astype(BF16)
    qs_ref[n:2 * n] = jnp.where(lane >= DQK, q, 0.0).astype(BF16)


def _softmax_step(s, vb, m_s, l_s, acc):
    m_prev = m_s[...]
    m_new = jnp.maximum(m_prev, jnp.max(s, axis=-1, keepdims=True))
    alpha = jnp.exp(m_prev - m_new)
    p = jnp.exp(s - m_new)
    l_s[...] = alpha * l_s[...] + jnp.sum(p, axis=-1, keepdims=True)
    acc[...] = alpha * acc[...] + _mm(p.astype(BF16), vb)
    m_s[...] = m_new


def _diff_finalize(lamv, g, l_s, acc, n, lam_init):
    lam = (jnp.exp(jnp.sum(lamv[0:1] * lamv[1:2], keepdims=True))
           - jnp.exp(jnp.sum(lamv[2:3] * lamv[3:4], keepdims=True)) + lam_init)
    o = acc[0:n] / l_s[0:n] - lam * (acc[n:2 * n] / l_s[n:2 * n])
    ms = jnp.mean(o * o, axis=-1, keepdims=True)
    return o * lax.rsqrt(ms + EPS) * g * (1.0 - lam_init)


def _attn_prompt_kernel(q_ref, k_ref, v_ref, lamv, gdn, o_ref, qs, m_s, l_s, acc, *, tq, lam_init):
    i = pl.program_id(2)
    j = pl.program_id(3)

    @pl.when(j == 0)
    def _():
        _stack_queries(q_ref[...], qs, tq)
        m_s[...] = jnp.full(m_s.shape, NEG, F32)
        l_s[...] = jnp.zeros(l_s.shape, F32)
        acc[...] = jnp.zeros(acc.shape, F32)

    def scores():
        return _mm_nt(qs[...], k_ref[...].astype(BF16))

    @pl.when(j < i)
    def _():
        _softmax_step(scores(), v_ref[...].astype(BF16), m_s, l_s, acc)

    @pl.when(j == i)
    def _():
        s = scores()
        row = lax.broadcasted_iota(jnp.int32, s.shape, 0) % tq
        col = lax.broadcasted_iota(jnp.int32, s.shape, 1)
        _softmax_step(jnp.where(col <= row, s, NEG), v_ref[...].astype(BF16), m_s, l_s, acc)
        o_ref[...] = _diff_finalize(lamv[...], gdn[...], l_s, acc, tq, lam_init).astype(o_ref.dtype)


def _attn_prompt(z, lamv, g_dn_l, n_batch, seq, lam_init):
    T = z.shape[0]
    tq = 512
    nq = seq // tq
    cb = 1024 // DVD
    vmem = 2 * 3 * tq * DVD * 4 + 2 * tq * DVD * 2 + 2 * tq * DVD * (2 + 4) + 4 * 2 * tq * LANES * 4 \
        + 6 * 2 * tq * tq * 4 + (8 << 20)
    return pl.pallas_call(
        functools.partial(_attn_prompt_kernel, tq=tq, lam_init=lam_init),
        grid=(n_batch, H_D, nq, nq),
        in_specs=[
            pl.BlockSpec((tq, DVD), lambda b, h, i, j: (b * nq + i, COL_DQ * cb + h)),
            pl.BlockSpec((tq, DVD), lambda b, h, i, j: (b * nq + jnp.minimum(i, j), COL_DK * cb + h)),
            pl.BlockSpec((tq, DVD), lambda b, h, i, j: (b * nq + jnp.minimum(i, j), COL_DV * cb + h)),
            pl.BlockSpec((4, DQK), lambda b, h, i, j: (0, 0)),
            pl.BlockSpec((None, 1, DVD), lambda b, h, i, j: (h, 0, 0)),
        ],
        out_specs=pl.BlockSpec((tq, DVD), lambda b, h, i, j: (b * nq + i, h)),
        out_shape=jax.ShapeDtypeStruct((T, W_D), BF16),
        scratch_shapes=[pltpu.VMEM((2 * tq, DVD), BF16), pltpu.VMEM((2 * tq, 1), F32),
                        pltpu.VMEM((2 * tq, 1), F32), pltpu.VMEM((2 * tq, DVD), F32)],
        compiler_params=_cparams(4, vmem),
        name="attn_prompt",
    )(z, z, z, lamv, g_dn_l[:, None, :])


PAGES_PER_STEP = 8


def _attn_sample_kernel(pt_ref, q_ref, *rest, n_steps, lam_init):
    del pt_ref
    kp = rest[:PAGES_PER_STEP]
    vp = rest[PAGES_PER_STEP:2 * PAGES_PER_STEP]
    kn, vn, bias, bias_new, lamv, gdn, o_ref, qs, m_s, l_s, acc = rest[2 * PAGES_PER_STEP:]
    n = q_ref.shape[0]
    step = pl.program_id(1)

    @pl.when(step == 0)
    def _():
        _stack_queries(q_ref[...], qs, n)
        m_s[...] = jnp.full(m_s.shape, NEG, F32)
        l_s[...] = jnp.zeros(l_s.shape, F32)
        acc[...] = jnp.zeros(acc.shape, F32)

    rows = PAGE_SIZE * H_D
    for p in range(PAGES_PER_STEP):
        kb = kp[p][...].reshape(rows, DVD).astype(BF16)
        vb = vp[p][...].reshape(rows, DVD).astype(BF16)
        _softmax_step(_mm_nt(qs[...], kb) + bias[...], vb, m_s, l_s, acc)

    @pl.when(step == n_steps - 1)
    def _():
        s = _mm_nt(qs[...], kn[...].astype(BF16)) + bias_new[...]
        _softmax_step(s, vn[...].astype(BF16), m_s, l_s, acc)
        o_ref[...] = _diff_finalize(lamv[...], gdn[...], l_s, acc, n, lam_init).astype(o_ref.dtype)


def _attn_sample(zs, cache_k, cache_v, page_table, l, lamv, g_dn_l, n_batch, n_tok, lam_init):
    n_pages = page_table.shape[1]
    n_steps = n_pages // PAGES_PER_STEP
    n = n_tok * H_D
    rows = PAGE_SIZE * H_D
    q = zs[:, COL_DQ * 1024:(COL_DQ + 1) * 1024].reshape(n_batch, n, DVD)
    pad = lambda t: jnp.pad(t.reshape(n_batch, n, DVD), ((0, 0), (0, LANES - n), (0, 0)))
    kn = pad(zs[:, COL_DK * 1024:(COL_DK + 1) * 1024])
    vn = pad(zs[:, COL_DV * 1024:(COL_DV + 1) * 1024])
    r = jnp.arange(2 * n)
    c = jnp.arange(rows)
    bias = jnp.where((r % H_D)[:, None] == (c % H_D)[None, :], 0.0, NEG).astype(F32)
    cn = jnp.arange(LANES)
    ok = ((r % H_D)[:, None] == (cn % H_D)[None, :]) & ((cn // H_D)[None, :] <= ((r % n) // H_D)[:, None]) \
        & (cn < n)[None, :]
    bias_new = jnp.where(ok, 0.0, NEG).astype(F32)
    gdn = jnp.tile(g_dn_l, (n_tok, 1))

    def page_spec(p):
        return pl.BlockSpec((None, None, PAGE_SIZE, H_D, DVD),
                            lambda b, s, pt: (l, pt[b * n_pages + s * PAGES_PER_STEP + p], 0, 0, 0))

    const = lambda shp: pl.BlockSpec(shp, lambda b, s, pt: (0,) * len(shp))
    per_b = lambda shp: pl.BlockSpec((None,) + shp, lambda b, s, pt: (b, 0, 0))
    page_bytes = rows * DVD * 4
    vmem = 2 * 2 * PAGES_PER_STEP * page_bytes + 2 * (2 * n) * rows * 4 + 8 * (2 * n) * rows * 4 \
        + 4 * rows * DVD * 2 + (8 << 20)
    out = pl.pallas_call(
        functools.partial(_attn_sample_kernel, n_steps=n_steps, lam_init=lam_init),
        grid_spec=pltpu.PrefetchScalarGridSpec(
            num_scalar_prefetch=1,
            grid=(n_batch, n_steps),
            in_specs=[per_b((n, DVD))]
            + [page_spec(p) for p in range(PAGES_PER_STEP)]
            + [page_spec(p) for p in range(PAGES_PER_STEP)]
            + [per_b((LANES, DVD)), per_b((LANES, DVD)), const((2 * n, rows)), const((2 * n, LANES)),
               const((4, DQK)), const((n, DVD))],
            out_specs=per_b((n, DVD)),
            scratch_shapes=[pltpu.VMEM((2 * n, DVD), BF16), pltpu.VMEM((2 * n, 1), F32),
                            pltpu.VMEM((2 * n, 1), F32), pltpu.VMEM((2 * n, DVD), F32)],
        ),
        out_shape=jax.ShapeDtypeStruct((n_batch, n, DVD), BF16),
        compiler_params=_cparams(2, vmem),
        name="attn_sample",
    )(page_table.reshape(-1), q, *([cache_k] * PAGES_PER_STEP), *([cache_v] * PAGES_PER_STEP),
      kn, vn, bias, bias_new, lamv, gdn)
    return out.reshape(n_batch * n_tok, W_D)


def _mix_kernel(yr, yd, ga, gb, wpa, wpb, o_ref):
    a = _mm(yr[...].astype(BF16), wpa[...].astype(BF16))
    d = _mm(yd[...].astype(BF16), wpb[...].astype(BF16))
    o_ref[...] = (jax.nn.sigmoid(ga[...]) * a + jax.nn.sigmoid(gb[...]) * d).astype(o_ref.dtype)


def _mix(yr, yd, z, w_pa, w_pb, l, tm):
    T = yr.shape[0]
    tn = 1024
    vmem = 2 * 2 * tm * W_R * 4 + 2 * 2 * tm * tn * 4 + 2 * 2 * W_R * tn * 4 + 2 * W_R * tn * 2 \
        + 2 * tm * tn * 2 + 4 * tm * tn * 4 + (6 << 20)
    return pl.pallas_call(
        _mix_kernel,
        grid=(T // tm, D_MODEL // tn),
        in_specs=[
            pl.BlockSpec((tm, W_R), lambda m, n: (m, 0)),
            pl.BlockSpec((tm, W_D), lambda m, n: (m, 0)),
            pl.BlockSpec((tm, tn), lambda m, n: (m, COL_GA + n)),
            pl.BlockSpec((tm, tn), lambda m, n: (m, COL_GB + n)),
            pl.BlockSpec((None, W_R, tn), lambda m, n: (l, 0, n)),
            pl.BlockSpec((None, W_D, tn), lambda m, n: (l, 0, n)),
        ],
        out_specs=pl.BlockSpec((tm, tn), lambda m, n: (m, n)),
        out_shape=jax.ShapeDtypeStruct((T, D_MODEL), BF16),
        compiler_params=_cparams(2, vmem),
        name="mix",
    )(yr, yd, z, z, w_pa, w_pb)


def _outproj_kernel(mix, wo, x_ref, gt, g2, sc, sh, x1_ref, h2_ref):
    x1 = x_ref[...] + gt[...] * _mm(mix[...], wo[...])
    x1_ref[...] = x1
    ms = jnp.mean(x1 * x1, axis=-1, keepdims=True)
    h = x1 * lax.rsqrt(ms + EPS) * g2[...]
    h2_ref[...] = (h * (1.0 + sc[...]) + sh[...]).astype(h2_ref.dtype)


def _outproj(mix, wo_b, x, mods, g2, tm, h2_dtype):
    T = x.shape[0]
    row = lambda: pl.BlockSpec((tm, D_MODEL), lambda m: (m, 0))
    vmem = 2 * tm * D_MODEL * (2 + 4 + 4 + 4) + 2 * D_MODEL * D_MODEL * 2 + 4 * tm * D_MODEL * 4 + (6 << 20)
    return pl.pallas_call(
        _outproj_kernel,
        grid=(T // tm,),
        in_specs=[row(), pl.BlockSpec((D_MODEL, D_MODEL), lambda m: (0, 0)), row(),
                  mods.spec(2, tm, 1), pl.BlockSpec((1, D_MODEL), lambda m: (0, 0)),
                  mods.spec(4, tm, 1), mods.spec(3, tm, 1)],
        out_specs=[row(), row()],
        out_shape=[jax.ShapeDtypeStruct((T, D_MODEL), F32), jax.ShapeDtypeStruct((T, D_MODEL), h2_dtype)],
        compiler_params=_cparams(1, vmem),
        name="outproj",
    )(mix, wo_b, x, mods.arr, g2, mods.arr, mods.arr)


def _ffn_up_kernel(te, first, nv, xs, w1, w3, o_ref, w1b, w3b):
    del te
    m = pl.program_id(1)

    @pl.when(first[m] == 1)
    def _():
        w1b[...] = w1[...].astype(BF16)
        w3b[...] = w3[...].astype(BF16)

    @pl.when(m < nv[0])
    def _():
        x = xs[...].astype(BF16)
        o_ref[...] = (_silu(_mm(x, w1b[...])) * _mm(x, w3b[...])).astype(o_ref.dtype)

    @pl.when(m >= nv[0])
    def _():
        o_ref[...] = jnp.zeros(o_ref.shape, o_ref.dtype)


def _ffn_up(xs, w1, w3, te, first, nv, tm, tf):
    P = xs.shape[0]
    ff = w1.shape[-1]
    n_mt = pl.cdiv(P, tm)
    xb = xs.dtype.itemsize
    vmem = 2 * tm * D_MODEL * xb + 2 * 2 * D_MODEL * tf * 4 + 2 * D_MODEL * tf * 2 + 2 * tm * tf * 2 \
        + 4 * tm * tf * 4 + tm * D_MODEL * 2 + (6 << 20)
    return pl.pallas_call(
        _ffn_up_kernel,
        grid_spec=pltpu.PrefetchScalarGridSpec(
            num_scalar_prefetch=3,
            grid=(pl.cdiv(ff, tf), n_mt),
            in_specs=[
                pl.BlockSpec((tm, D_MODEL), lambda f, m, te, fi, nv: (m, 0)),
                pl.BlockSpec((None, D_MODEL, tf), lambda f, m, te, fi, nv: (te[m], 0, f)),
                pl.BlockSpec((None, D_MODEL, tf), lambda f, m, te, fi, nv: (te[m], 0, f)),
            ],
            out_specs=pl.BlockSpec((tm, tf), lambda f, m, te, fi, nv: (m, f)),
            scratch_shapes=[pltpu.VMEM((D_MODEL, tf), BF16), pltpu.VMEM((D_MODEL, tf), BF16)],
        ),
        out_shape=jax.ShapeDtypeStruct((P, ff), BF16),
        compiler_params=_cparams(2, vmem),
        name="ffn_up",
    )(te, first, nv, xs, w1, w3)


def _ffn_down_kernel(te, first, nv, act, w2, *rest, accumulate):
    del te
    if accumulate:
        y_in, o_ref, w2b = rest
    else:
        o_ref, w2b = rest
    m = pl.program_id(1)

    @pl.when(first[m] == 1)
    def _():
        w2b[...] = w2[...].astype(BF16)

    @pl.when(m < nv[0])
    def _():
        y = _mm(act[...], w2b[...])
        o_ref[...] = y_in[...] + y if accumulate else y

    @pl.when(m >= nv[0])
    def _():
        o_ref[...] = y_in[...] if accumulate else jnp.zeros(o_ref.shape, o_ref.dtype)


def _ffn_down_slab(act, w2, te, first, nv, tm, tn, tk, k, y_prev):
    P = act.shape[0]
    n_mt = pl.cdiv(P, tm)
    accumulate = y_prev is not None
    in_specs = [
        pl.BlockSpec((tm, tk), lambda n, m, te, fi, nv: (m, k)),
        pl.BlockSpec((None, tk, tn), lambda n, m, te, fi, nv: (te[m], k, n)),
    ]
    args = [act, w2]
    if accumulate:
        in_specs.append(pl.BlockSpec((tm, tn), lambda n, m, te, fi, nv: (m, n)))
        args.append(y_prev)
    vmem = 2 * tm * tk * 2 + 2 * tk * tn * 4 + 2 * tk * tn * 2 + 6 * tm * tn * 4 + (6 << 20)
    return pl.pallas_call(
        functools.partial(_ffn_down_kernel, accumulate=accumulate),
        grid_spec=pltpu.PrefetchScalarGridSpec(
            num_scalar_prefetch=3,
            grid=(D_MODEL // tn, n_mt),
            in_specs=in_specs,
            out_specs=pl.BlockSpec((tm, tn), lambda n, m, te, fi, nv: (m, n)),
            scratch_shapes=[pltpu.VMEM((tk, tn), BF16)],
        ),
        out_shape=jax.ShapeDtypeStruct((P, D_MODEL), F32),
        compiler_params=_cparams(2, vmem),
        name="ffn_down",
    )(te, first, nv, *args)


def _ffn_down(act, w2, te, first, nv, tm, tn, n_k):
    tk = act.shape[1] // n_k
    y = None
    for k in range(n_k):
        y = _ffn_down_slab(act, w2, te, first, nv, tm, tn, tk, k, y)
    return y


def _single_expert_tiles(P, tm, e):
    n_mt = pl.cdiv(P, tm)
    te = jnp.full((n_mt,), e, jnp.int32)
    first = jnp.zeros((n_mt,), jnp.int32).at[0].set(1)
    return te, first, jnp.full((1,), n_mt, jnp.int32)


def _residual_kernel(x1, gt, y, o_ref):
    o_ref[...] = x1[...] + gt[...] * y[...]


def _residual(x1, mods, y, tm):
    T = x1.shape[0]
    row = lambda: pl.BlockSpec((tm, D_MODEL), lambda m: (m, 0))
    return pl.pallas_call(
        _residual_kernel,
        grid=(T // tm,),
        in_specs=[row(), mods.spec(5, tm, 1), row()],
        out_specs=row(),
        out_shape=jax.ShapeDtypeStruct((T, D_MODEL), F32),
        compiler_params=_cparams(1, 8 * tm * D_MODEL * 4 + (4 << 20)),
        name="residual",
    )(x1, mods.arr, y)


def _router_kernel(h_ref, wr_ref, o_ref):
    logits = jnp.dot(h_ref[...], wr_ref[...], preferred_element_type=F32, precision=lax.Precision.HIGHEST)
    lane = lax.broadcasted_iota(jnp.int32, logits.shape, 1)
    logits = jnp.where(lane < N_EXPERTS, logits, NEG)
    v1 = jnp.max(logits, axis=-1, keepdims=True)
    i1 = jnp.min(jnp.where(logits == v1, lane, LANES), axis=-1, keepdims=True)
    rest = jnp.where(lane == i1, NEG, logits)
    v2 = jnp.max(rest, axis=-1, keepdims=True)
    i2 = jnp.min(jnp.where(rest == v2, lane, LANES), axis=-1, keepdims=True)
    e2 = jnp.exp(v2 - v1)
    w1 = 1.0 / (1.0 + e2)
    w2 = e2 / (1.0 + e2)
    o_ref[...] = jnp.where(lane == 0, i1.astype(F32),
                           jnp.where(lane == 1, i2.astype(F32),
                                     jnp.where(lane == 2, w1, jnp.where(lane == 3, w2, 0.0))))


def _router(h2, w_router_l, tm):
    T = h2.shape[0]
    wr = jnp.pad(w_router_l, ((0, 0), (0, LANES - N_EXPERTS)))
    return pl.pallas_call(
        _router_kernel,
        grid=(pl.cdiv(T, tm),),
        in_specs=[pl.BlockSpec((tm, D_MODEL), lambda m: (m, 0)),
                  pl.BlockSpec((D_MODEL, LANES), lambda m: (0, 0))],
        out_specs=pl.BlockSpec((tm, LANES), lambda m: (m, 0)),
        out_shape=jax.ShapeDtypeStruct((T, LANES), F32),
        compiler_params=_cparams(1, 6 * tm * D_MODEL * 4 + (8 << 20)),
        name="router",
    )(h2, wr)


def _row_copy(src_hbm, row, dst_vmem, slot, sem):
    return pltpu.make_async_copy(src_hbm.at[pl.ds(row, 1)], dst_vmem.at[pl.ds(slot, 1)], sem)


def _dispatch_kernel(src, h_hbm, o_ref, buf, sem, *, tm):
    base = pl.program_id(0) * tm

    def start(r, carry):
        _row_copy(h_hbm, src[base + r], buf, r, sem).start()
        return carry

    def wait(r, carry):
        _row_copy(h_hbm, 0, buf, r, sem).wait()
        return carry

    lax.fori_loop(0, tm, start, 0)
    lax.fori_loop(0, tm, wait, 0)
    o_ref[...] = buf[...].astype(o_ref.dtype)


def _dispatch(h2, src, tm):
    P = src.shape[0]
    return pl.pallas_call(
        functools.partial(_dispatch_kernel, tm=tm),
        grid_spec=pltpu.PrefetchScalarGridSpec(
            num_scalar_prefetch=1,
            grid=(P // tm,),
            in_specs=[pl.BlockSpec(memory_space=pl.ANY)],
            out_specs=pl.BlockSpec((tm, D_MODEL), lambda m, src: (m, 0)),
            scratch_shapes=[pltpu.VMEM((tm, D_MODEL), F32), pltpu.SemaphoreType.DMA(())],
        ),
        out_shape=jax.ShapeDtypeStruct((P, D_MODEL), BF16),
        compiler_params=_cparams(1, 2 * tm * D_MODEL * 2 + 3 * tm * D_MODEL * 4 + (4 << 20)),
        name="dispatch",
    )(src, h2)


def _combine_kernel(pos, y_hbm, rout, x1, gt, o_ref, buf, sem, *, tm):
    base = pl.program_id(0) * tm

    def start(r, carry):
        for s in range(TOP_K):
            _row_copy(y_hbm, pos[(base + r) * TOP_K + s], buf.at[s], r, sem).start()
        return carry

    def wait(r, carry):
        for s in range(TOP_K):
            _row_copy(y_hbm, 0, buf.at[s], r, sem).wait()
        return carry

    lax.fori_loop(0, tm, start, 0)
    lax.fori_loop(0, tm, wait, 0)
    g = rout[...]
    ffn = g[:, 2:3] * buf[0] + g[:, 3:4] * buf[1]
    o_ref[...] = x1[...] + gt[...] * ffn


def _combine(y, pos, rout, x1, mods, tm):
    T = x1.shape[0]
    row = lambda: pl.BlockSpec((tm, D_MODEL), lambda m, pos: (m, 0))
    return pl.pallas_call(
        functools.partial(_combine_kernel, tm=tm),
        grid_spec=pltpu.PrefetchScalarGridSpec(
            num_scalar_prefetch=1,
            grid=(T // tm,),
            in_specs=[pl.BlockSpec(memory_space=pl.ANY),
                      pl.BlockSpec((tm, LANES), lambda m, pos: (m, 0)),
                      row(), mods.spec(5, tm, 2)],
            out_specs=row(),
            scratch_shapes=[pltpu.VMEM((TOP_K, tm, D_MODEL), F32), pltpu.SemaphoreType.DMA(())],
        ),
        out_shape=jax.ShapeDtypeStruct((T, D_MODEL), F32),
        compiler_params=_cparams(1, (TOP_K + 8) * tm * D_MODEL * 4 + (4 << 20)),
        name="combine",
    )(pos, y, rout, x1, mods.arr)


def _route_tables(rout, tm, n_mt):
    n_tok = rout.shape[0]
    ef = rout[:, :TOP_K].astype(jnp.int32).reshape(-1)
    onehot = (ef[:, None] == jnp.arange(N_EXPERTS)[None, :]).astype(jnp.int32)
    counts = jnp.sum(onehot, axis=0)
    padded = ((counts + tm - 1) // tm) * tm
    ends = jnp.cumsum(padded)
    offs = ends - padded
    rank = jnp.sum((jnp.cumsum(onehot, axis=0) - onehot) * onehot, axis=1)
    pos = (offs[ef] + rank).astype(jnp.int32)
    src = jnp.zeros((n_mt * tm,), jnp.int32).at[pos].set(jnp.arange(n_tok * TOP_K, dtype=jnp.int32) // TOP_K)
    n_valid = (ends[-1] // tm).astype(jnp.int32)
    starts = jnp.arange(n_mt, dtype=jnp.int32) * tm
    te = jnp.minimum(jnp.searchsorted(ends, starts, side="right"), N_EXPERTS - 1).astype(jnp.int32)
    te = jnp.where(jnp.arange(n_mt) < n_valid, te, te[jnp.maximum(n_valid - 1, 0)])
    first = jnp.concatenate([jnp.ones((1,), jnp.int32), (te[1:] != te[:-1]).astype(jnp.int32)])
    return pos, src, te, first, n_valid.reshape(1)


def kernel(x_prompt, x_sample, cache_k, cache_v, state_ret, page_table, c_prompt, c_sample, w_ada, b_ada,
           g_norm1, g_norm2, w_in, g_qn, g_kn, lam_q1, lam_k1, lam_q2, lam_k2, g_ret, g_dn, w_pa, w_pb, w_o,
           w_ff1, w_ff3, w_ff2, w_router, w_e1, w_e3, w_e2):
    B, S, _ = x_prompt.shape
    Bd, Sd, _ = x_sample.shape
    n_pages = page_table.shape[1]
    past = n_pages * cache_k.shape[2]
    Tp, Ts = B * S, Bd * Sd
    tm_p = 512

    c_all = jnp.concatenate([c_prompt, c_sample], axis=0)
    c_all = jnp.pad(c_all, ((0, (-c_all.shape[0]) % 8), (0, 0)))
    mod = _adaln(c_all, w_ada, b_ada).reshape(DEPTH, c_all.shape[0], 6, D_MODEL)

    pos_p = jnp.arange(S)
    pos_s = past + jnp.arange(Sd)
    r0_p = jnp.zeros((B, H_R, DR, DR), F32)
    n_exp_w = w_e1.shape[0] * N_EXPERTS
    we1 = w_e1.reshape(n_exp_w, D_MODEL, -1)
    we3 = w_e3.reshape(n_exp_w, D_MODEL, -1)
    we2 = w_e2.reshape(n_exp_w, -1, D_MODEL)

    xp = x_prompt.reshape(Tp, D_MODEL)
    xs = x_sample.reshape(Ts, D_MODEL)
    kp, vp, rp, ksl, vsl, rsl = [], [], [], [], [], []
    for l in range(DEPTH):
        lam_init = 0.8 - 0.6 * math.exp(-0.3 * l)
        lamv = jnp.stack([lam_q1[l], lam_k1[l], lam_q2[l], lam_k2[l]]).astype(F32)
        mods_p = _Mods(mod[l, :B], "prompt", S)
        mods_s = _Mods(mod[l, B:B + Bd], "sample", Sd)
        wo_b = w_o[l].astype(BF16)
        g2 = g_norm2[l][None]
        moe = l % 2 == 1
        h2_dtype = F32 if moe else BF16

        zp = _inproj(xp, mods_p, l, g_norm1, w_in, g_qn, g_kn, 1024)
        yr_p, r_new_p = _retention(zp, pos_p, r0_p, g_ret[l], B, RET_CHUNK, BF16)
        yd_p = _attn_prompt(zp, lamv, g_dn[l], B, S, lam_init)
        mix_p = _mix(yr_p, yd_p, zp, w_pa, w_pb, l, tm_p)
        x1_p, h2_p = _outproj(mix_p, wo_b, xp, mods_p, g2, tm_p, h2_dtype)
        kp.append(zp[:, COL_DK * 1024:(COL_DK + 1) * 1024].reshape(B, S, H_D, 2 * DQK))
        vp.append(zp[:, COL_DV * 1024:(COL_DV + 1) * 1024].reshape(B, S, H_D, DVD))
        rp.append(r_new_p)

        zs = _inproj(xs, mods_s, l, g_norm1, w_in, g_qn, g_kn, Ts)
        yr_s, r_new_s = _retention(zs, pos_s, state_ret[l], g_ret[l], Bd, Sd, F32)
        yd_s = _attn_sample(zs, cache_k, cache_v, page_table, l, lamv, g_dn[l], Bd, Sd, lam_init)
        mix_s = _mix(yr_s, yd_s, zs, w_pa, w_pb, l, Ts)
        x1_s, h2_s = _outproj(mix_s, wo_b, xs, mods_s, g2, Ts, h2_dtype)
        ksl.append(zs[:, COL_DK * 1024:(COL_DK + 1) * 1024].reshape(Bd, Sd, H_D, 2 * DQK))
        vsl.append(zs[:, COL_DV * 1024:(COL_DV + 1) * 1024].reshape(Bd, Sd, H_D, DVD))
        rsl.append(r_new_s)

        i = l // 2
        if not moe:
            outs = []
            for h2, x1, mods, tm in ((h2_p, x1_p, mods_p, tm_p), (h2_s, x1_s, mods_s, Ts)):
                te, first, nv = _single_expert_tiles(h2.shape[0], tm, i)
                act = _ffn_up(h2, w_ff1, w_ff3, te, first, nv, tm, 512)
                y = _ffn_down(act, w_ff2, te, first, nv, tm, 512, 1)
                outs.append(_residual(x1, mods, y, tm))
            xp, xs = outs
        else:
            tm_e = 512
            h2_all = jnp.concatenate([h2_p, h2_s], axis=0)
            n_tok = Tp + Ts
            n_mt = (n_tok * TOP_K + N_EXPERTS * (tm_e - 1)) // tm_e + 1
            rout = _router(h2_all, w_router[i], 512)
            pos, src, te, first, nv = _route_tables(rout, tm_e, n_mt)
            te = te + i * N_EXPERTS
            xs_sorted = _dispatch(h2_all, src, tm_e)
            act = _ffn_up(xs_sorted, we1, we3, te, first, nv, tm_e, 512)
            y = _ffn_down(act, we2, te, first, nv, tm_e, 512, 2)
            xp = _combine(y, pos[:Tp * TOP_K], rout[:Tp], x1_p, mods_p, 256)
            xs = _combine(y, pos[Tp * TOP_K:], rout[Tp:], x1_s, mods_s, Ts)

    return (xp.reshape(B, S, D_MODEL), xs.reshape(Bd, Sd, D_MODEL),
            jnp.stack(kp), jnp.stack(vp), jnp.stack(rp), jnp.stack(ksl), jnp.stack(vsl), jnp.stack(rsl))
```

```python
import functools
import math

import jax
import jax.numpy as jnp
from jax import lax
from jax.experimental import pallas as pl
from jax.experimental.pallas import tpu as pltpu

F32 = jnp.float32
BF16 = jnp.bfloat16

D_MODEL = 2048
DEPTH = 4
PAGE_SIZE = 128
H_R = 8
DR = 128
W_R = H_R * DR
RET_CHUNK = 128
ROPE_BASE = 10000.0
H_D = 8
DQK = 64
DVD = 2 * DQK
W_DQ = H_D * 2 * DQK
W_D = H_D * DVD
N_IN = 4 * W_R + 2 * W_DQ + W_D + 2 * D_MODEL
N_EXPERTS = 8
TOP_K = 2
EPS = 1e-6
NEG = -1e30
LOG2E = math.log2(math.e)

LANES = 128
VMEM_CAP_BYTES = 60000 * 1024

COL_RQ, COL_RK, COL_RV, COL_RG, COL_DQ, COL_DK, COL_DV, COL_GA, COL_GB = 0, 1, 2, 3, 4, 5, 6, 7, 9


def _cparams(n_axes, vmem_bytes):
    return pltpu.CompilerParams(
        dimension_semantics=("arbitrary",) * n_axes,
        vmem_limit_bytes=min(int(vmem_bytes), VMEM_CAP_BYTES))


def _mm(a, b):
    return jnp.dot(a, b, preferred_element_type=F32)


def _mm_nt(a, b):
    return lax.dot_general(a, b, (((1,), (1,)), ((), ())), preferred_element_type=F32)


def _mm_tn(a, b):
    return lax.dot_general(a, b, (((0,), (0,)), ((), ())), preferred_element_type=F32)


def _silu(x):
    return x * jax.nn.sigmoid(x)


def _parts(x, precise):
    hi = x.astype(BF16)
    if not precise or x.dtype == BF16:
        return (hi,)
    return (hi, (x - hi.astype(F32)).astype(BF16))


def _dot(a_parts, b_parts, mm=_mm):
    out = None
    for i, a in enumerate(a_parts):
        for j, b in enumerate(b_parts):
            if i + j < 2:
                t = mm(a, b)
                out = t if out is None else out + t
    return out


def _n_parts(precise):
    return 2 if precise else 1


def _load_parts(ref):
    return tuple(ref[p] for p in range(ref.shape[0]))


def _store_parts(ref, x):
    for p, part in enumerate(_parts(x, ref.shape[0] == 2)):
        ref[p] = part


ROW_CHUNKS = D_MODEL // LANES


def _store_chunked(ref3, value):
    for c in range(value.shape[1] // LANES):
        ref3[:, c, :] = value[:, c * LANES:(c + 1) * LANES]


def _adaln_kernel(c_ref, w_ref, b_ref, o_ref):
    o_ref[...] = _dot(_parts(_silu(c_ref[...]), True), _parts(w_ref[...], True)) + b_ref[...]


def _adaln(c_all, w_ada, b_ada):
    rows = c_all.shape[0]
    tn = 1536
    n_out = 6 * D_MODEL
    return pl.pallas_call(
        _adaln_kernel,
        grid=(DEPTH, n_out // tn),
        in_specs=[
            pl.BlockSpec((rows, D_MODEL), lambda l, n: (0, 0)),
            pl.BlockSpec((None, D_MODEL, tn), lambda l, n: (l, 0, n)),
            pl.BlockSpec((None, 1, tn), lambda l, n: (l, 0, n)),
        ],
        out_specs=pl.BlockSpec((None, rows, tn), lambda l, n: (l, 0, n)),
        out_shape=jax.ShapeDtypeStruct((DEPTH, rows, n_out), F32),
        compiler_params=_cparams(2, 2 * D_MODEL * tn * 4 + 4 * D_MODEL * tn * 2 + (8 << 20)),
        name="adaln",
    )(c_all, w_ada, b_ada.reshape(DEPTH, 1, n_out))


class _Mods:
    def __init__(self, mod_l, group, rows_per_batch=None):
        self.group = group
        if group == "prompt":
            self.arr = mod_l.reshape(mod_l.shape[0], 6, 1, D_MODEL)
            self.rows_per_batch = rows_per_batch
        else:
            n_rep = rows_per_batch
            self.arr = jnp.transpose(jnp.repeat(mod_l, n_rep, axis=0), (1, 0, 2))

    def spec(self, j, tm):
        if self.group == "prompt":
            per = self.rows_per_batch // tm
            return pl.BlockSpec((None, None, 1, D_MODEL), lambda *g: (g[0] // per, j, 0, 0))
        return pl.BlockSpec((None, self.arr.shape[1], D_MODEL), lambda *g: (j, 0, 0))


def _inproj_kernel(x_ref, g_ref, sc_ref, sh_ref, w_ref, gq_ref, gk_ref, bd_ref, o_ref, k_ref, v_ref, h_scr, *, tn):
    n = pl.program_id(1)
    precise = h_scr.shape[0] == 2

    @pl.when(n == 0)
    def _():
        x = x_ref[...]
        ms = jnp.mean(x * x, axis=-1, keepdims=True)
        h = x * lax.rsqrt(ms + EPS) * g_ref[...]
        _store_parts(h_scr, h * (1.0 + sc_ref[...]) + sh_ref[...])

    z = _dot(_load_parts(h_scr), _parts(w_ref[...], precise))
    per = 1024 // tn
    is_q = (n >= COL_DQ * per) & (n < (COL_DQ + 1) * per)
    is_k = (n >= COL_DK * per) & (n < (COL_DK + 1) * per)
    is_v = (n >= COL_DV * per) & (n < (COL_DV + 1) * per)
    normed = is_q | is_k

    @pl.when(normed)
    def _():
        ms = _dot(_parts(z * z, precise), (bd_ref[...],))
        zn = z * lax.rsqrt(ms + EPS) * jnp.where(is_q, gq_ref[...], gk_ref[...])
        o_ref[...] = zn

        @pl.when(is_k)
        def _():
            k_ref[...] = zn

    @pl.when(jnp.logical_not(normed))
    def _():
        o_ref[...] = z

    @pl.when(is_v)
    def _():
        v_ref[...] = z


def _inproj(x, mods, l, g_norm1, w_in, g_qn, g_kn, tm, precise=False):
    T = x.shape[0]
    tn = 512
    per = 1024 // tn
    grp = jnp.arange(tn) // DQK
    bd = jnp.where(grp[:, None] == grp[None, :], 1.0 / DQK, 0.0).astype(BF16)
    gq = jnp.tile(g_qn[l], tn // DQK)[None]
    gk = jnp.tile(g_kn[l], tn // DQK)[None]

    def kv_spec(col):
        return pl.BlockSpec((tm, tn), lambda m, n: (m, jnp.clip(n - col * per, 0, per - 1)))

    n_p = _n_parts(precise)
    vmem = (2 * tm * D_MODEL * 4 + 2 * D_MODEL * tn * 4 + 6 * tm * tn * 4 + n_p * tm * D_MODEL * 2
            + (2 * n_p - 1) * D_MODEL * tn * 2 + 3 * tm * tn * 4 + 2 * tn * tn * 2 + (6 << 20))
    return pl.pallas_call(
        functools.partial(_inproj_kernel, tn=tn),
        grid=(T // tm, N_IN // tn),
        in_specs=[
            pl.BlockSpec((tm, D_MODEL), lambda m, n: (m, 0)),
            pl.BlockSpec((1, D_MODEL), lambda m, n: (0, 0)),
            mods.spec(1, tm),
            mods.spec(0, tm),
            pl.BlockSpec((None, D_MODEL, tn), lambda m, n: (l, 0, n)),
            pl.BlockSpec((1, tn), lambda m, n: (0, 0)),
            pl.BlockSpec((1, tn), lambda m, n: (0, 0)),
            pl.BlockSpec((tn, tn), lambda m, n: (0, 0)),
        ],
        out_specs=[pl.BlockSpec((tm, tn), lambda m, n: (m, n)), kv_spec(COL_DK), kv_spec(COL_DV)],
        out_shape=[jax.ShapeDtypeStruct((T, N_IN), F32), jax.ShapeDtypeStruct((T, W_DQ), F32),
                   jax.ShapeDtypeStruct((T, W_D), F32)],
        scratch_shapes=[pltpu.VMEM((n_p, tm, D_MODEL), BF16)],
        compiler_params=_cparams(2, vmem),
        name="inproj",
    )(x, g_norm1[l][None], mods.arr, mods.arr, w_in, gq, gk, bd)


def _retention_kernel(zq, zk, zv, zg, cos_ref, sin_ref, dmat, xi, zeta, gc, gr, r0, yr, rout, r_scr, *, nc, pr):
    c = pl.program_id(1)

    @pl.when(c == 0)
    def _():
        r_scr[...] = r0[...]

    cs = cos_ref[...]
    sn = sin_ref[...]
    for h in range(H_R):
        sl = slice(h * DR, (h + 1) * DR)
        q = zq[:, sl]
        k = zk[:, sl]
        q = q * cs + pltpu.roll(q, DR // 2, 1) * sn
        k = (k * cs + pltpu.roll(k, DR // 2, 1) * sn) * (DR ** -0.5)
        vb = _parts(zv[:, sl], pr)
        s = _dot(_parts(q, pr), _parts(k, pr), _mm_nt) * dmat[h]
        r_h = r_scr[h]
        o = _dot(_parts(s, pr), vb) + _dot(_parts(q * xi[h], pr), _parts(r_h, pr))
        r_scr[h] = gc[h] * r_h + _dot(_parts(k * zeta[h], pr), vb, _mm_tn)
        oc = o - jnp.mean(o, axis=-1, keepdims=True)
        y = oc * lax.rsqrt(jnp.mean(oc * oc, axis=-1, keepdims=True) + EPS) * gr[h]
        yr[:, sl] = (y * _silu(zg[:, sl])).astype(yr.dtype)

    @pl.when(c == nc - 1)
    def _():
        rout[...] = r_scr[...]


def _retention_consts(C):
    log_g = jnp.log1p(-jnp.exp2(-5.0 - jnp.arange(H_R, dtype=F32)))
    idx = jnp.arange(C, dtype=F32)
    diff = idx[:, None] - idx[None, :]
    causal = diff >= 0
    dmat = jnp.where(causal[None], jnp.exp(log_g[:, None, None] * jnp.where(causal, diff, 0.0)[None]), 0.0)
    xi = jnp.exp(log_g[:, None] * (idx[None, :] + 1.0))
    zeta = jnp.exp(log_g[:, None] * (C - 1.0 - idx)[None, :])
    gc = jnp.exp(log_g * C)
    bc = lambda t: jnp.broadcast_to(t[:, :, None], (H_R, C, DR))
    return dmat, bc(xi), bc(zeta), jnp.broadcast_to(gc[:, None, None], (H_R, 1, DR))


def _rope_tables(pos):
    half = DR // 2
    inv = 1.0 / (ROPE_BASE ** (jnp.arange(half, dtype=F32) / half))
    ang = pos.astype(F32)[:, None] * inv[None, :]
    cos, sin = jnp.cos(ang), jnp.sin(ang)
    return jnp.concatenate([cos, cos], -1), jnp.concatenate([-sin, sin], -1)


def _retention(z, pos, r0, g_ret_l, n_batch, C, out_dtype, precise=False):
    T = z.shape[0]
    nc = T // (n_batch * C)
    cos, sin = _rope_tables(pos)
    dmat, xi, zeta, gc = _retention_consts(C)
    zspec = lambda col: pl.BlockSpec((C, W_R), lambda b, c: (b * nc + c, col))
    full = lambda shp: pl.BlockSpec(shp, lambda b, c: (0,) * len(shp))
    state_spec = pl.BlockSpec((None, H_R, DR, DR), lambda b, c: (b, 0, 0, 0))
    vmem = 2 * 5 * C * W_R * 4 + 2 * (H_R * C * C + 2 * H_R * C * DR) * 4 + 5 * H_R * DR * DR * 4 + (8 << 20)
    return pl.pallas_call(
        functools.partial(_retention_kernel, nc=nc, pr=precise),
        grid=(n_batch, nc),
        in_specs=[zspec(COL_RQ), zspec(COL_RK), zspec(COL_RV), zspec(COL_RG),
                  pl.BlockSpec((C, DR), lambda b, c: (c, 0)),
                  pl.BlockSpec((C, DR), lambda b, c: (c, 0)),
                  full((H_R, C, C)), full((H_R, C, DR)), full((H_R, C, DR)), full((H_R, 1, DR)),
                  full((H_R, 1, DR)), state_spec],
        out_specs=[pl.BlockSpec((C, W_R), lambda b, c: (b * nc + c, 0)), state_spec],
        out_shape=[jax.ShapeDtypeStruct((T, W_R), out_dtype),
                   jax.ShapeDtypeStruct((n_batch, H_R, DR, DR), F32)],
        scratch_shapes=[pltpu.VMEM((H_R, DR, DR), F32)],
        compiler_params=_cparams(2, vmem),
        name="retention",
    )(z, z, z, z, cos, sin, dmat, xi, zeta, gc, g_ret_l[:, None, :], r0)


def _stack_queries(q, qs_ref, n):
    del n
    lane = lax.broadcasted_iota(jnp.int32, q.shape, 1)
    q = q * (DQK ** -0.5 * LOG2E)
    _store_parts(qs_ref, jnp.concatenate([jnp.where(lane < DQK, q, 0.0), jnp.where(lane >= DQK, q, 0.0)], axis=0))


def _value_parts(v, precise):
    parts = _parts(v, precise)
    lane = lax.broadcasted_iota(jnp.int32, v.shape, 1)
    ones = jnp.where(lane == 0, 1.0, 0.0).astype(BF16)
    tails = (ones,) + (jnp.zeros(v.shape, BF16),) * (len(parts) - 1)
    return tuple(jnp.concatenate([p, t], axis=1) for p, t in zip(parts, tails))


def _softmax_update(score_blocks, value_blocks, m_ref, acc_ref, rows, precise=False):
    m_prev = m_ref[rows]
    m_new = m_prev
    for s in score_blocks:
        m_new = jnp.maximum(m_new, jnp.max(s, axis=-1, keepdims=True))
    pv = None
    for s, va in zip(score_blocks, value_blocks):
        t = _dot(_parts(jnp.exp2(s - m_new), precise), va)
        pv = t if pv is None else pv + t
    acc_ref[rows] = jnp.exp2(m_prev - m_new) * acc_ref[rows] + pv
    m_ref[rows] = m_new


def _diff_finalize(lamv, g, acc, n, lam_init):
    lam = (jnp.exp(jnp.sum(lamv[0:1] * lamv[1:2], keepdims=True))
           - jnp.exp(jnp.sum(lamv[2:3] * lamv[3:4], keepdims=True)) + lam_init)
    o1 = acc[0:n, 0:DVD] / acc[0:n, DVD:DVD + 1]
    o2 = acc[n:2 * n, 0:DVD] / acc[n:2 * n, DVD:DVD + 1]
    o = o1 - lam * o2
    ms = jnp.mean(o * o, axis=-1, keepdims=True)
    return o * lax.rsqrt(ms + EPS) * g * (1.0 - lam_init)


def _attn_init(q, qs, m_s, acc, n):
    _stack_queries(q, qs, n)
    m_s[...] = jnp.full(m_s.shape, NEG, F32)
    acc[...] = jnp.zeros(acc.shape, F32)


def _attn_prompt_kernel(it, jt, q_ref, k_ref, v_ref, lamv, gdn, o_ref, qs, m_s, acc, *, tq, rc, lam_init):
    t = pl.program_id(2)
    i = it[t]
    j = jt[t]

    @pl.when(j == 0)
    def _():
        _attn_init(q_ref[...], qs, m_s, acc, tq)

    def update(masked):
        kb = k_ref[...].astype(BF16)
        va = _value_parts(v_ref[...], False)
        for c in range(2 * tq // rc):
            rows = slice(c * rc, (c + 1) * rc)
            s = _mm_nt(qs[0, rows], kb)
            if masked:
                row = lax.broadcasted_iota(jnp.int32, s.shape, 0) + (c * rc) % tq
                col = lax.broadcasted_iota(jnp.int32, s.shape, 1)
                s = jnp.where(col <= row, s, NEG)
            _softmax_update([s], [va], m_s, acc, rows)

    @pl.when(j < i)
    def _():
        update(False)

    @pl.when(j == i)
    def _():
        update(True)
        o_ref[...] = _diff_finalize(lamv[...], gdn[...], acc, tq, lam_init).astype(o_ref.dtype)


def _attn_prompt(z, k, v, lamv, g_dn_l, n_batch, seq, lam_init):
    T = z.shape[0]
    tq = 512
    rc = 256
    nq = seq // tq
    cb = 1024 // DVD
    pairs = [(i, j) for i in range(nq) for j in range(i + 1)]
    it = jnp.asarray([p[0] for p in pairs], jnp.int32)
    jt = jnp.asarray([p[1] for p in pairs], jnp.int32)
    vmem = 2 * 3 * tq * DVD * 4 + 2 * tq * DVD * 2 + 2 * tq * DVD * 2 + 2 * tq * 2 * DVD * 4 \
        + 2 * tq * LANES * 4 + 8 * rc * tq * 4 + (8 << 20)
    return pl.pallas_call(
        functools.partial(_attn_prompt_kernel, tq=tq, rc=rc, lam_init=lam_init),
        grid_spec=pltpu.PrefetchScalarGridSpec(
            num_scalar_prefetch=2,
            grid=(n_batch, H_D, len(pairs)),
            in_specs=[
                pl.BlockSpec((tq, DVD), lambda b, h, t, it, jt: (b * nq + it[t], COL_DQ * cb + h)),
                pl.BlockSpec((tq, DVD), lambda b, h, t, it, jt: (b * nq + jt[t], h)),
                pl.BlockSpec((tq, DVD), lambda b, h, t, it, jt: (b * nq + jt[t], h)),
                pl.BlockSpec((4, DQK), lambda b, h, t, it, jt: (0, 0)),
                pl.BlockSpec((None, 1, DVD), lambda b, h, t, it, jt: (h, 0, 0)),
            ],
            out_specs=pl.BlockSpec((tq, DVD), lambda b, h, t, it, jt: (b * nq + it[t], h)),
            scratch_shapes=[pltpu.VMEM((1, 2 * tq, DVD), BF16), pltpu.VMEM((2 * tq, 1), F32),
                            pltpu.VMEM((2 * tq, 2 * DVD), F32)],
        ),
        out_shape=jax.ShapeDtypeStruct((T, W_D), BF16),
        compiler_params=_cparams(3, vmem),
        name="attn_prompt",
    )(it, jt, z, k, v, lamv, g_dn_l[:, None, :])


PAGES_PER_STEP = 8


def _attn_sample_kernel(pt_ref, q_ref, *rest, n_steps, lam_init):
    del pt_ref
    kp = rest[:PAGES_PER_STEP]
    vp = rest[PAGES_PER_STEP:2 * PAGES_PER_STEP]
    kn, vn, bias, bias_new, lamv, gdn, o_ref, qs, m_s, acc = rest[2 * PAGES_PER_STEP:]
    n = q_ref.shape[0]
    step = pl.program_id(1)
    every_row = slice(0, 2 * n)
    pr = qs.shape[0] == 2

    @pl.when(step == 0)
    def _():
        _attn_init(q_ref[...], qs, m_s, acc, n)

    rows = PAGE_SIZE * H_D
    q = _load_parts(qs)
    scores = [_dot(q, _parts(kp[p][...].reshape(rows, DVD), pr), _mm_nt) + bias[...] for p in range(PAGES_PER_STEP)]
    values = [_value_parts(vp[p][...].reshape(rows, DVD), pr) for p in range(PAGES_PER_STEP)]
    _softmax_update(scores, values, m_s, acc, every_row, pr)

    @pl.when(step == n_steps - 1)
    def _():
        s = _dot(q, _parts(kn[...], pr), _mm_nt) + bias_new[...]
        _softmax_update([s], [_value_parts(vn[...], pr)], m_s, acc, every_row, pr)
        o_ref[...] = _diff_finalize(lamv[...], gdn[...], acc, n, lam_init).astype(o_ref.dtype)


def _attn_sample(zs, ks, vs, cache_k, cache_v, page_table, l, lamv, g_dn_l, n_batch, n_tok, lam_init, precise=True):
    n_pages = page_table.shape[1]
    n_steps = n_pages // PAGES_PER_STEP
    n = n_tok * H_D
    rows = PAGE_SIZE * H_D
    q = zs[:, COL_DQ * 1024:(COL_DQ + 1) * 1024].reshape(n_batch, n, DVD)
    pad = lambda t: jnp.pad(t.reshape(n_batch, n, DVD), ((0, 0), (0, LANES - n), (0, 0)))
    kn = pad(ks)
    vn = pad(vs)
    r = jnp.arange(2 * n)
    c = jnp.arange(rows)
    bias = jnp.where((r % H_D)[:, None] == (c % H_D)[None, :], 0.0, NEG).astype(F32)
    cn = jnp.arange(LANES)
    ok = ((r % H_D)[:, None] == (cn % H_D)[None, :]) & ((cn // H_D)[None, :] <= ((r % n) // H_D)[:, None]) \
        & (cn < n)[None, :]
    bias_new = jnp.where(ok, 0.0, NEG).astype(F32)
    gdn = jnp.tile(g_dn_l, (n_tok, 1))

    def page_spec(p):
        return pl.BlockSpec((None, None, PAGE_SIZE, H_D, DVD),
                            lambda b, s, pt: (l, pt[b * n_pages + s * PAGES_PER_STEP + p], 0, 0, 0))

    const = lambda shp: pl.BlockSpec(shp, lambda b, s, pt: (0,) * len(shp))
    per_b = lambda shp: pl.BlockSpec((None,) + shp, lambda b, s, pt: (b, 0, 0))
    page_bytes = rows * DVD * 4
    n_p = _n_parts(precise)
    vmem = 2 * 2 * PAGES_PER_STEP * page_bytes + 2 * (2 * n) * rows * 4 \
        + (1 + n_p) * PAGES_PER_STEP * (2 * n) * rows * 4 + n_p * PAGES_PER_STEP * rows * 3 * DVD * 2 + (8 << 20)
    out = pl.pallas_call(
        functools.partial(_attn_sample_kernel, n_steps=n_steps, lam_init=lam_init),
        grid_spec=pltpu.PrefetchScalarGridSpec(
            num_scalar_prefetch=1,
            grid=(n_batch, n_steps),
            in_specs=[per_b((n, DVD))]
            + [page_spec(p) for p in range(PAGES_PER_STEP)]
            + [page_spec(p) for p in range(PAGES_PER_STEP)]
            + [per_b((LANES, DVD)), per_b((LANES, DVD)), const((2 * n, rows)), const((2 * n, LANES)),
               const((4, DQK)), const((n, DVD))],
            out_specs=per_b((n, DVD)),
            scratch_shapes=[pltpu.VMEM((n_p, 2 * n, DVD), BF16), pltpu.VMEM((2 * n, 1), F32),
                            pltpu.VMEM((2 * n, 2 * DVD), F32)],
        ),
        out_shape=jax.ShapeDtypeStruct((n_batch, n, DVD), F32 if precise else BF16),
        compiler_params=_cparams(2, vmem),
        name="attn_sample",
    )(page_table.reshape(-1), q, *([cache_k] * PAGES_PER_STEP), *([cache_v] * PAGES_PER_STEP),
      kn, vn, bias, bias_new, lamv, gdn)
    return out.reshape(n_batch * n_tok, W_D)


def _mix_kernel(yr, yd, ga, gb, wpa, wpb, o_ref, *, pr):
    a = _dot(_parts(yr[...], pr), _parts(wpa[...], pr))
    d = _dot(_parts(yd[...], pr), _parts(wpb[...], pr))
    o_ref[...] = (jax.nn.sigmoid(ga[...]) * a + jax.nn.sigmoid(gb[...]) * d).astype(o_ref.dtype)


def _mix(yr, yd, z, w_pa, w_pb, l, tm, precise=False):
    T = yr.shape[0]
    tn = 1024
    vmem = 2 * 2 * tm * W_R * 4 + 2 * 2 * tm * tn * 4 + 2 * 2 * W_R * tn * 4 + 4 * W_R * tn * 2 \
        + 2 * tm * tn * 4 + 4 * tm * tn * 4 + (6 << 20)
    return pl.pallas_call(
        functools.partial(_mix_kernel, pr=precise),
        grid=(T // tm, D_MODEL // tn),
        in_specs=[
            pl.BlockSpec((tm, W_R), lambda m, n: (m, 0)),
            pl.BlockSpec((tm, W_D), lambda m, n: (m, 0)),
            pl.BlockSpec((tm, tn), lambda m, n: (m, COL_GA + n)),
            pl.BlockSpec((tm, tn), lambda m, n: (m, COL_GB + n)),
            pl.BlockSpec((None, W_R, tn), lambda m, n: (l, 0, n)),
            pl.BlockSpec((None, W_D, tn), lambda m, n: (l, 0, n)),
        ],
        out_specs=pl.BlockSpec((tm, tn), lambda m, n: (m, n)),
        out_shape=jax.ShapeDtypeStruct((T, D_MODEL), F32 if precise else BF16),
        compiler_params=_cparams(2, vmem),
        name="mix",
    )(yr, yd, z, z, w_pa, w_pb)


def _outproj_kernel(mix, wo, x_ref, gt, g2, sc, sh, x1_ref, h2_ref):
    pr = wo.dtype == F32
    x1 = x_ref[...] + gt[...] * _dot(_parts(mix[...], pr), _parts(wo[...], pr))
    x1_ref[...] = x1
    ms = jnp.mean(x1 * x1, axis=-1, keepdims=True)
    h = x1 * lax.rsqrt(ms + EPS) * g2[...]
    h = (h * (1.0 + sc[...]) + sh[...]).astype(h2_ref.dtype)
    if len(h2_ref.shape) == 3:
        _store_chunked(h2_ref, h)
    else:
        h2_ref[...] = h


def _outproj(mix, wo, l, x, mods, g2, tm, chunked_h2):
    T = x.shape[0]
    precise = wo.dtype == F32
    row = lambda: pl.BlockSpec((tm, D_MODEL), lambda m: (m, 0))
    if chunked_h2:
        h2_spec = pl.BlockSpec((tm, ROW_CHUNKS, LANES), lambda m: (m, 0, 0))
        h2_shape = jax.ShapeDtypeStruct((T, ROW_CHUNKS, LANES), F32)
    else:
        h2_spec, h2_shape = row(), jax.ShapeDtypeStruct((T, D_MODEL), F32 if precise else BF16)
    vmem = 2 * tm * D_MODEL * (4 + 4 + 4 + 4) + 2 * D_MODEL * D_MODEL * wo.dtype.itemsize \
        + (2 * D_MODEL * D_MODEL * 2 if precise else 0) + 4 * tm * D_MODEL * 4 + (6 << 20)
    return pl.pallas_call(
        _outproj_kernel,
        grid=(T // tm,),
        in_specs=[row(), pl.BlockSpec((None, D_MODEL, D_MODEL), lambda m: (l, 0, 0)), row(),
                  mods.spec(2, tm), pl.BlockSpec((1, D_MODEL), lambda m: (0, 0)),
                  mods.spec(4, tm), mods.spec(3, tm)],
        out_specs=[row(), h2_spec],
        out_shape=[jax.ShapeDtypeStruct((T, D_MODEL), F32), h2_shape],
        compiler_params=_cparams(1, vmem),
        name="outproj",
    )(mix, wo, x, mods.arr, g2, mods.arr, mods.arr)


def _ffn_up_kernel(te, first, nv, xs, w1, w3, o_ref, w1b, w3b):
    del te
    m = pl.program_id(1)

    @pl.when(first[m] == 1)
    def _():
        _store_parts(w1b, w1[...])
        _store_parts(w3b, w3[...])

    @pl.when(m < nv[0])
    def _():
        x = _parts(xs[...], w1b.shape[0] == 2)
        o_ref[...] = (_silu(_dot(x, _load_parts(w1b))) * _dot(x, _load_parts(w3b))).astype(o_ref.dtype)

    @pl.when(m >= nv[0])
    def _():
        o_ref[...] = jnp.zeros(o_ref.shape, o_ref.dtype)


def _ffn_up(xs, w1, w3, te, first, nv, tm, tf, precise=False):
    P = xs.shape[0]
    ff = w1.shape[-1]
    n_mt = pl.cdiv(P, tm)
    xb = xs.dtype.itemsize
    n_p = _n_parts(precise)
    vmem = 2 * tm * D_MODEL * xb + 2 * 2 * D_MODEL * tf * 4 + 2 * n_p * D_MODEL * tf * 2 + 2 * tm * tf * 4 \
        + 4 * tm * tf * 4 + n_p * tm * D_MODEL * 2 + (6 << 20)
    return pl.pallas_call(
        _ffn_up_kernel,
        grid_spec=pltpu.PrefetchScalarGridSpec(
            num_scalar_prefetch=3,
            grid=(pl.cdiv(ff, tf), n_mt),
            in_specs=[
                pl.BlockSpec((tm, D_MODEL), lambda f, m, te, fi, nv: (m, 0)),
                pl.BlockSpec((None, D_MODEL, tf), lambda f, m, te, fi, nv: (te[m], 0, f)),
                pl.BlockSpec((None, D_MODEL, tf), lambda f, m, te, fi, nv: (te[m], 0, f)),
            ],
            out_specs=pl.BlockSpec((tm, tf), lambda f, m, te, fi, nv: (m, f)),
            scratch_shapes=[pltpu.VMEM((n_p, D_MODEL, tf), BF16), pltpu.VMEM((n_p, D_MODEL, tf), BF16)],
        ),
        out_shape=jax.ShapeDtypeStruct((P, ff), F32 if precise else BF16),
        compiler_params=_cparams(2, vmem),
        name="ffn_up",
    )(te, first, nv, xs, w1, w3)


def _ffn_down_kernel(te, first, nv, act, w2, o_ref, w2b):
    del te
    m = pl.program_id(1)

    @pl.when(first[m] == 1)
    def _():
        _store_parts(w2b, w2[...])

    @pl.when(m < nv[0])
    def _():
        o_ref[...] = _dot(_parts(act[...], w2b.shape[0] == 2), _load_parts(w2b))

    @pl.when(m >= nv[0])
    def _():
        o_ref[...] = jnp.zeros(o_ref.shape, o_ref.dtype)


def _ffn_down(act, w2, te, first, nv, tm, tn, precise=False):
    P, ff = act.shape
    n_mt = pl.cdiv(P, tm)
    n_p = _n_parts(precise)
    vmem = (2 + n_p) * tm * ff * act.dtype.itemsize + 2 * ff * tn * 4 + n_p * ff * tn * 2 + 4 * tm * tn * 4 + (4 << 20)
    return pl.pallas_call(
        _ffn_down_kernel,
        grid_spec=pltpu.PrefetchScalarGridSpec(
            num_scalar_prefetch=3,
            grid=(D_MODEL // tn, n_mt),
            in_specs=[
                pl.BlockSpec((tm, ff), lambda n, m, te, fi, nv: (m, 0)),
                pl.BlockSpec((None, ff, tn), lambda n, m, te, fi, nv: (te[m], 0, n)),
            ],
            out_specs=pl.BlockSpec((tm, tn), lambda n, m, te, fi, nv: (m, n)),
            scratch_shapes=[pltpu.VMEM((n_p, ff, tn), BF16)],
        ),
        out_shape=jax.ShapeDtypeStruct((P, D_MODEL), F32),
        compiler_params=_cparams(2, vmem),
        name="ffn_down",
    )(te, first, nv, act, w2)


def _single_expert_tiles(P, tm, e):
    n_mt = pl.cdiv(P, tm)
    te = jnp.full((n_mt,), e, jnp.int32)
    first = jnp.zeros((n_mt,), jnp.int32).at[0].set(1)
    return te, first, jnp.full((1,), n_mt, jnp.int32)


def _split_tiles(te, first, nv, parts):
    rest = jnp.zeros((first.shape[0], parts - 1), jnp.int32)
    return jnp.repeat(te, parts), jnp.concatenate([first[:, None], rest], axis=1).reshape(-1), nv * parts


def _residual_kernel(x1, gt, y, o_ref):
    o_ref[...] = x1[...] + gt[...] * y[...]


def _residual(x1, mods, y, tm):
    T = x1.shape[0]
    row = lambda: pl.BlockSpec((tm, D_MODEL), lambda m: (m, 0))
    return pl.pallas_call(
        _residual_kernel,
        grid=(T // tm,),
        in_specs=[row(), mods.spec(5, tm), row()],
        out_specs=row(),
        out_shape=jax.ShapeDtypeStruct((T, D_MODEL), F32),
        compiler_params=_cparams(1, 8 * tm * D_MODEL * 4 + (4 << 20)),
        name="residual",
    )(x1, mods.arr, y)


def _router_kernel(h_ref, wr_ref, o_ref):
    logits = None
    for c in range(ROW_CHUNKS):
        part = jnp.dot(h_ref[:, c, :], wr_ref[c * LANES:(c + 1) * LANES, :], preferred_element_type=F32,
                       precision=lax.Precision.HIGHEST)
        logits = part if logits is None else logits + part
    lane = lax.broadcasted_iota(jnp.int32, logits.shape, 1)
    logits = jnp.where(lane < N_EXPERTS, logits, NEG)
    v1 = jnp.max(logits, axis=-1, keepdims=True)
    i1 = jnp.min(jnp.where(logits == v1, lane, LANES), axis=-1, keepdims=True)
    rest = jnp.where(lane == i1, NEG, logits)
    v2 = jnp.max(rest, axis=-1, keepdims=True)
    i2 = jnp.min(jnp.where(rest == v2, lane, LANES), axis=-1, keepdims=True)
    e2 = jnp.exp(v2 - v1)
    w1 = 1.0 / (1.0 + e2)
    w2 = e2 / (1.0 + e2)
    o_ref[...] = jnp.where(lane == 0, i1.astype(F32),
                           jnp.where(lane == 1, i2.astype(F32),
                                     jnp.where(lane == 2, w1, jnp.where(lane == 3, w2, 0.0))))


def _router(h2, w_router_l, tm):
    T = h2.shape[0]
    wr = jnp.pad(w_router_l, ((0, 0), (0, LANES - N_EXPERTS)))
    return pl.pallas_call(
        _router_kernel,
        grid=(pl.cdiv(T, tm),),
        in_specs=[pl.BlockSpec((tm, ROW_CHUNKS, LANES), lambda m: (m, 0, 0)),
                  pl.BlockSpec((D_MODEL, LANES), lambda m: (0, 0))],
        out_specs=pl.BlockSpec((tm, LANES), lambda m: (m, 0)),
        out_shape=jax.ShapeDtypeStruct((T, LANES), F32),
        compiler_params=_cparams(1, 6 * tm * D_MODEL * 4 + (8 << 20)),
        name="router",
    )(h2, wr)


DMA_UNROLL = 8


def _row_copy(src_hbm, row, dst_vmem, slot, sem):
    return pltpu.make_async_copy(src_hbm.at[pl.ds(row, 1)], dst_vmem.at[pl.ds(slot, 1)], sem)


def _dispatch_kernel(src, h_hbm, o_ref, buf, sem, *, tm):
    base = pl.program_id(0) * tm

    def start(r, carry):
        _row_copy(h_hbm, src[base + r], buf, r, sem).start()
        return carry

    def wait(r, carry):
        _row_copy(h_hbm, 0, buf, r, sem).wait()
        return carry

    lax.fori_loop(0, tm, start, 0, unroll=DMA_UNROLL)
    lax.fori_loop(0, tm, wait, 0, unroll=DMA_UNROLL)
    for c in range(ROW_CHUNKS):
        o_ref[:, c * LANES:(c + 1) * LANES] = buf[:, c, :].astype(o_ref.dtype)


def _dispatch(h2, src, tm, out_dtype):
    P = src.shape[0]
    return pl.pallas_call(
        functools.partial(_dispatch_kernel, tm=tm),
        grid_spec=pltpu.PrefetchScalarGridSpec(
            num_scalar_prefetch=1,
            grid=(P // tm,),
            in_specs=[pl.BlockSpec(memory_space=pl.ANY)],
            out_specs=pl.BlockSpec((tm, D_MODEL), lambda m, src: (m, 0)),
            scratch_shapes=[pltpu.VMEM((tm, ROW_CHUNKS, LANES), F32), pltpu.SemaphoreType.DMA(())],
        ),
        out_shape=jax.ShapeDtypeStruct((P, D_MODEL), out_dtype),
        compiler_params=_cparams(1, 2 * tm * D_MODEL * 4 + 3 * tm * D_MODEL * 4 + (4 << 20)),
        name="dispatch",
    )(src, h2)


def _combine_kernel(pos, y_hbm, rout, x1, gt, o_ref, buf, sem, *, tm):
    base = pl.program_id(0) * tm

    def start(r, carry):
        for s in range(TOP_K):
            _row_copy(y_hbm, pos[(base + r) * TOP_K + s], buf.at[s], r, sem).start()
        return carry

    def wait(r, carry):
        for s in range(TOP_K):
            _row_copy(y_hbm, 0, buf.at[s], r, sem).wait()
        return carry

    lax.fori_loop(0, tm, start, 0, unroll=DMA_UNROLL)
    lax.fori_loop(0, tm, wait, 0, unroll=DMA_UNROLL)
    g = rout[...]
    ffn = g[:, 2:3] * buf[0] + g[:, 3:4] * buf[1]
    o_ref[...] = x1[...] + gt[...] * ffn


def _combine(y, pos, rout, x1, mods, tm):
    T = x1.shape[0]
    row = lambda: pl.BlockSpec((tm, D_MODEL), lambda m, pos: (m, 0))
    return pl.pallas_call(
        functools.partial(_combine_kernel, tm=tm),
        grid_spec=pltpu.PrefetchScalarGridSpec(
            num_scalar_prefetch=1,
            grid=(T // tm,),
            in_specs=[pl.BlockSpec(memory_space=pl.ANY),
                      pl.BlockSpec((tm, LANES), lambda m, pos: (m, 0)),
                      row(), mods.spec(5, tm)],
            out_specs=row(),
            scratch_shapes=[pltpu.VMEM((TOP_K, tm, D_MODEL), F32), pltpu.SemaphoreType.DMA(())],
        ),
        out_shape=jax.ShapeDtypeStruct((T, D_MODEL), F32),
        compiler_params=_cparams(1, (TOP_K + 8) * tm * D_MODEL * 4 + (4 << 20)),
        name="combine",
    )(pos, y, rout, x1, mods.arr)


def _route_tables(rout, tm, n_mt):
    n_tok = rout.shape[0]
    ef = rout[:, :TOP_K].astype(jnp.int32).reshape(-1)
    onehot = (ef[:, None] == jnp.arange(N_EXPERTS)[None, :]).astype(jnp.int32)
    counts = jnp.sum(onehot, axis=0)
    padded = ((counts + tm - 1) // tm) * tm
    ends = jnp.cumsum(padded)
    offs = ends - padded
    rank = jnp.sum((jnp.cumsum(onehot, axis=0) - onehot) * onehot, axis=1)
    pos = (offs[ef] + rank).astype(jnp.int32)
    src = jnp.zeros((n_mt * tm,), jnp.int32).at[pos].set(jnp.arange(n_tok * TOP_K, dtype=jnp.int32) // TOP_K)
    n_valid = (ends[-1] // tm).astype(jnp.int32)
    starts = jnp.arange(n_mt, dtype=jnp.int32) * tm
    te = jnp.minimum(jnp.searchsorted(ends, starts, side="right"), N_EXPERTS - 1).astype(jnp.int32)
    te = jnp.where(jnp.arange(n_mt) < n_valid, te, te[jnp.maximum(n_valid - 1, 0)])
    first = jnp.concatenate([jnp.ones((1,), jnp.int32), (te[1:] != te[:-1]).astype(jnp.int32)])
    return pos, src, te, first, n_valid.reshape(1)


def kernel(x_prompt, x_sample, cache_k, cache_v, state_ret, page_table, c_prompt, c_sample, w_ada, b_ada,
           g_norm1, g_norm2, w_in, g_qn, g_kn, lam_q1, lam_k1, lam_q2, lam_k2, g_ret, g_dn, w_pa, w_pb, w_o,
           w_ff1, w_ff3, w_ff2, w_router, w_e1, w_e3, w_e2):
    B, S, _ = x_prompt.shape
    Bd, Sd, _ = x_sample.shape
    n_pages = page_table.shape[1]
    past = n_pages * cache_k.shape[2]
    Tp, Ts = B * S, Bd * Sd
    tm_p = 512

    c_all = jnp.concatenate([c_prompt, c_sample], axis=0)
    c_all = jnp.pad(c_all, ((0, (-c_all.shape[0]) % 8), (0, 0)))
    mod = _adaln(c_all, w_ada, b_ada).reshape(DEPTH, c_all.shape[0], 6, D_MODEL)

    pos_p = jnp.arange(S)
    pos_s = past + jnp.arange(Sd)
    r0_p = jnp.zeros((B, H_R, DR, DR), F32)
    n_exp_w = w_e1.shape[0] * N_EXPERTS
    we1 = w_e1.reshape(n_exp_w, D_MODEL, -1)
    we3 = w_e3.reshape(n_exp_w, D_MODEL, -1)
    we2 = w_e2.reshape(n_exp_w, -1, D_MODEL)

    xp = x_prompt.reshape(Tp, D_MODEL)
    xs = x_sample.reshape(Ts, D_MODEL)
    kp, vp, rp, ksl, vsl, rsl = [], [], [], [], [], []
    for l in range(DEPTH):
        lam_init = 0.8 - 0.6 * math.exp(-0.3 * l)
        lamv = jnp.stack([lam_q1[l], lam_k1[l], lam_q2[l], lam_k2[l]]).astype(F32)
        mods_p = _Mods(mod[l, :B], "prompt", S)
        mods_s = _Mods(mod[l, B:B + Bd], "sample", Sd)
        wo_b = w_o[l].astype(BF16)[None]
        g2 = g_norm2[l][None]
        moe = l % 2 == 1

        zp, k_p, v_p = _inproj(xp, mods_p, l, g_norm1, w_in, g_qn, g_kn, 1024)
        yr_p, r_new_p = _retention(zp, pos_p, r0_p, g_ret[l], B, RET_CHUNK, BF16)
        yd_p = _attn_prompt(zp, k_p, v_p, lamv, g_dn[l], B, S, lam_init)
        mix_p = _mix(yr_p, yd_p, zp, w_pa, w_pb, l, tm_p)
        x1_p, h2_p = _outproj(mix_p, wo_b, 0, xp, mods_p, g2, tm_p, moe)
        kp.append(k_p.reshape(B, S, H_D, 2 * DQK))
        vp.append(v_p.reshape(B, S, H_D, DVD))
        rp.append(r_new_p)

        zs, k_s, v_s = _inproj(xs, mods_s, l, g_norm1, w_in, g_qn, g_kn, Ts, precise=True)
        yr_s, r_new_s = _retention(zs, pos_s, state_ret[l], g_ret[l], Bd, Sd, F32, precise=True)
        yd_s = _attn_sample(zs, k_s, v_s, cache_k, cache_v, page_table, l, lamv, g_dn[l], Bd, Sd, lam_init)
        mix_s = _mix(yr_s, yd_s, zs, w_pa, w_pb, l, Ts, precise=True)
        x1_s, h2_s = _outproj(mix_s, w_o, l, xs, mods_s, g2, Ts, moe)
        ksl.append(k_s.reshape(Bd, Sd, H_D, 2 * DQK))
        vsl.append(v_s.reshape(Bd, Sd, H_D, DVD))
        rsl.append(r_new_s)

        i = l // 2
        if not moe:
            outs = []
            for h2, x1, mods, tm, pr in ((h2_p, x1_p, mods_p, tm_p, False), (h2_s, x1_s, mods_s, Ts, True)):
                te, first, nv = _single_expert_tiles(h2.shape[0], tm, i)
                act = _ffn_up(h2, w_ff1, w_ff3, te, first, nv, tm, 512, pr)
                y = _ffn_down(act, w_ff2, te, first, nv, tm, 512, pr)
                outs.append(_residual(x1, mods, y, tm))
            xp, xs = outs
        else:
            outs = []
            for h2, x1, mods, tm_e, tm_c, pr in ((h2_p, x1_p, mods_p, 512, 256, False),
                                                 (h2_s, x1_s, mods_s, Ts, Ts, True)):
                n_tok = h2.shape[0]
                n_mt = (n_tok * TOP_K + N_EXPERTS * (tm_e - 1)) // tm_e + 1
                rout = _router(h2, w_router[i], min(n_tok, 512))
                pos, src, te, first, nv = _route_tables(rout, tm_e, n_mt)
                te = te + i * N_EXPERTS
                xs_sorted = _dispatch(h2, src, tm_e, F32 if pr else BF16)
                act = _ffn_up(xs_sorted, we1, we3, te, first, nv, tm_e, 512, pr)
                parts = 1 if pr else 2
                y = _ffn_down(act, we2, *_split_tiles(te, first, nv, parts), tm_e // parts, 512, pr)
                outs.append(_combine(y, pos, rout, x1, mods, tm_c))
            xp, xs = outs

    return (xp.reshape(B, S, D_MODEL), xs.reshape(Bd, Sd, D_MODEL),
            jnp.stack(kp), jnp.stack(vp), jnp.stack(rp), jnp.stack(ksl), jnp.stack(vsl), jnp.stack(rsl))
```

```python
import functools
import math

import jax
import jax.numpy as jnp
from jax import lax
from jax.experimental import pallas as pl
from jax.experimental.pallas import tpu as pltpu

F32 = jnp.float32
BF16 = jnp.bfloat16

D_MODEL = 2048
DEPTH = 4
PAGE_SIZE = 128
H_R = 8
DR = 128
W_R = H_R * DR
RET_CHUNK = 128
ROPE_BASE = 10000.0
H_D = 8
DQK = 64
DVD = 2 * DQK
W_DQ = H_D * 2 * DQK
W_D = H_D * DVD
N_IN = 4 * W_R + 2 * W_DQ + W_D + 2 * D_MODEL
N_EXPERTS = 8
TOP_K = 2
EPS = 1e-6
NEG = -1e30
LOG2E = math.log2(math.e)

LANES = 128
VMEM_CAP_BYTES = 60000 * 1024

COL_RQ, COL_RK, COL_RV, COL_RG, COL_DQ, COL_DK, COL_DV, COL_GA, COL_GB = 0, 1, 2, 3, 4, 5, 6, 7, 9


def _cparams(n_axes, vmem_bytes):
    return pltpu.CompilerParams(
        dimension_semantics=("arbitrary",) * n_axes,
        vmem_limit_bytes=min(int(vmem_bytes), VMEM_CAP_BYTES))


def _mm(a, b):
    return jnp.dot(a, b, preferred_element_type=F32)


def _mm_nt(a, b):
    return lax.dot_general(a, b, (((1,), (1,)), ((), ())), preferred_element_type=F32)


def _mm_tn(a, b):
    return lax.dot_general(a, b, (((0,), (0,)), ((), ())), preferred_element_type=F32)


def _silu(x):
    return x * jax.nn.sigmoid(x)


def _parts(x, precise):
    hi = x.astype(BF16)
    if not precise or x.dtype == BF16:
        return (hi,)
    return (hi, (x - hi.astype(F32)).astype(BF16))


def _dot(a_parts, b_parts, mm=_mm):
    rows = a_parts[0].shape[0]
    if len(a_parts) == 2 and mm is not _mm_tn and rows % 16 == 0:
        both = mm(jnp.concatenate(a_parts, axis=0), b_parts[0])
        out = both[:rows] + both[rows:]
        return out + mm(a_parts[0], b_parts[1]) if len(b_parts) == 2 else out
    out = None
    for i, a in enumerate(a_parts):
        for j, b in enumerate(b_parts):
            if i + j < 2:
                t = mm(a, b)
                out = t if out is None else out + t
    return out


def _n_parts(precise):
    return 2 if precise else 1


def _load_parts(ref):
    return tuple(ref[p] for p in range(ref.shape[0]))


def _store_parts(ref, x):
    for p, part in enumerate(_parts(x, ref.shape[0] == 2)):
        ref[p] = part


ROW_CHUNKS = D_MODEL // LANES


def _store_chunked(ref3, value):
    for c in range(value.shape[1] // LANES):
        ref3[:, c, :] = value[:, c * LANES:(c + 1) * LANES]


def _adaln_kernel(c_ref, w_ref, b_ref, o_ref):
    o_ref[...] = _dot(_parts(_silu(c_ref[...]), True), _parts(w_ref[...], True)) + b_ref[...]


def _adaln(c_all, w_ada, b_ada):
    rows = c_all.shape[0]
    tn = 1536
    n_out = 6 * D_MODEL
    return pl.pallas_call(
        _adaln_kernel,
        grid=(DEPTH, n_out // tn),
        in_specs=[
            pl.BlockSpec((rows, D_MODEL), lambda l, n: (0, 0)),
            pl.BlockSpec((None, D_MODEL, tn), lambda l, n: (l, 0, n)),
            pl.BlockSpec((None, 1, tn), lambda l, n: (l, 0, n)),
        ],
        out_specs=pl.BlockSpec((None, rows, tn), lambda l, n: (l, 0, n)),
        out_shape=jax.ShapeDtypeStruct((DEPTH, rows, n_out), F32),
        compiler_params=_cparams(2, 2 * D_MODEL * tn * 4 + 4 * D_MODEL * tn * 2 + (8 << 20)),
        name="adaln",
    )(c_all, w_ada, b_ada.reshape(DEPTH, 1, n_out))


class _Mods:
    def __init__(self, mod_l, group, rows_per_batch=None):
        self.group = group
        if group == "prompt":
            self.arr = mod_l.reshape(mod_l.shape[0], 6, 1, D_MODEL)
            self.rows_per_batch = rows_per_batch
        else:
            n_rep = rows_per_batch
            self.arr = jnp.transpose(jnp.repeat(mod_l, n_rep, axis=0), (1, 0, 2))

    def spec(self, j, tm):
        if self.group == "prompt":
            per = self.rows_per_batch // tm
            return pl.BlockSpec((None, None, 1, D_MODEL), lambda *g: (g[0] // per, j, 0, 0))
        return pl.BlockSpec((None, self.arr.shape[1], D_MODEL), lambda *g: (j, 0, 0))


def _inproj_kernel(x_ref, g_ref, sc_ref, sh_ref, w_ref, gq_ref, gk_ref, bd_ref, o_ref, k_ref, v_ref, h_scr, *, tn):
    n = pl.program_id(1)
    precise = h_scr.shape[0] == 2

    @pl.when(n == 0)
    def _():
        x = x_ref[...]
        ms = jnp.mean(x * x, axis=-1, keepdims=True)
        h = x * lax.rsqrt(ms + EPS) * g_ref[...]
        _store_parts(h_scr, h * (1.0 + sc_ref[...]) + sh_ref[...])

    z = _dot(_load_parts(h_scr), _parts(w_ref[...], precise))
    per = 1024 // tn
    is_q = (n >= COL_DQ * per) & (n < (COL_DQ + 1) * per)
    is_k = (n >= COL_DK * per) & (n < (COL_DK + 1) * per)
    is_v = (n >= COL_DV * per) & (n < (COL_DV + 1) * per)
    normed = is_q | is_k

    @pl.when(normed)
    def _():
        ms = _dot(_parts(z * z, precise), (bd_ref[...],))
        zn = z * lax.rsqrt(ms + EPS) * jnp.where(is_q, gq_ref[...], gk_ref[...])
        o_ref[...] = zn

        @pl.when(is_k)
        def _():
            k_ref[...] = zn

    @pl.when(jnp.logical_not(normed))
    def _():
        o_ref[...] = z

    @pl.when(is_v)
    def _():
        v_ref[...] = z


def _inproj(x, mods, l, g_norm1, w_in, g_qn, g_kn, tm, precise=False):
    T = x.shape[0]
    tn = 512
    per = 1024 // tn
    grp = jnp.arange(tn) // DQK
    bd = jnp.where(grp[:, None] == grp[None, :], 1.0 / DQK, 0.0).astype(BF16)
    gq = jnp.tile(g_qn[l], tn // DQK)[None]
    gk = jnp.tile(g_kn[l], tn // DQK)[None]

    def kv_spec(col):
        return pl.BlockSpec((tm, tn), lambda m, n: (m, jnp.clip(n - col * per, 0, per - 1)))

    n_p = _n_parts(precise)
    vmem = (2 * tm * D_MODEL * 4 + 2 * D_MODEL * tn * 4 + 6 * tm * tn * 4 + n_p * tm * D_MODEL * 2
            + (2 * n_p - 1) * D_MODEL * tn * 2 + 3 * tm * tn * 4 + 2 * tn * tn * 2 + (6 << 20))
    return pl.pallas_call(
        functools.partial(_inproj_kernel, tn=tn),
        grid=(T // tm, N_IN // tn),
        in_specs=[
            pl.BlockSpec((tm, D_MODEL), lambda m, n: (m, 0)),
            pl.BlockSpec((1, D_MODEL), lambda m, n: (0, 0)),
            mods.spec(1, tm),
            mods.spec(0, tm),
            pl.BlockSpec((None, D_MODEL, tn), lambda m, n: (l, 0, n)),
            pl.BlockSpec((1, tn), lambda m, n: (0, 0)),
            pl.BlockSpec((1, tn), lambda m, n: (0, 0)),
            pl.BlockSpec((tn, tn), lambda m, n: (0, 0)),
        ],
        out_specs=[pl.BlockSpec((tm, tn), lambda m, n: (m, n)), kv_spec(COL_DK), kv_spec(COL_DV)],
        out_shape=[jax.ShapeDtypeStruct((T, N_IN), F32), jax.ShapeDtypeStruct((T, W_DQ), F32),
                   jax.ShapeDtypeStruct((T, W_D), F32)],
        scratch_shapes=[pltpu.VMEM((n_p, tm, D_MODEL), BF16)],
        compiler_params=_cparams(2, vmem),
        name="inproj",
    )(x, g_norm1[l][None], mods.arr, mods.arr, w_in, gq, gk, bd)


def _retention_kernel(zq, zk, zv, zg, cos_ref, sin_ref, dmat, xi, zeta, gc, gr, r0, yr, rout, r_scr, *, nc, pr):
    c = pl.program_id(1)

    @pl.when(c == 0)
    def _():
        r_scr[...] = r0[...]

    cs = cos_ref[...]
    sn = sin_ref[...]
    for h in range(H_R):
        sl = slice(h * DR, (h + 1) * DR)
        q = zq[:, sl]
        k = zk[:, sl]
        q = q * cs + pltpu.roll(q, DR // 2, 1) * sn
        k = (k * cs + pltpu.roll(k, DR // 2, 1) * sn) * (DR ** -0.5)
        vb = _parts(zv[:, sl], pr)
        s = _dot(_parts(q, pr), _parts(k, pr), _mm_nt) * dmat[h]
        r_h = r_scr[h]
        o = _dot(_parts(s, pr), vb) + _dot(_parts(q * xi[h], pr), _parts(r_h, pr))
        r_scr[h] = gc[h] * r_h + _dot(_parts(k * zeta[h], pr), vb, _mm_tn)
        oc = o - jnp.mean(o, axis=-1, keepdims=True)
        y = oc * lax.rsqrt(jnp.mean(oc * oc, axis=-1, keepdims=True) + EPS) * gr[h]
        yr[:, sl] = (y * _silu(zg[:, sl])).astype(yr.dtype)

    @pl.when(c == nc - 1)
    def _():
        rout[...] = r_scr[...]


def _retention_consts(C):
    log_g = jnp.log1p(-jnp.exp2(-5.0 - jnp.arange(H_R, dtype=F32)))
    idx = jnp.arange(C, dtype=F32)
    diff = idx[:, None] - idx[None, :]
    causal = diff >= 0
    dmat = jnp.where(causal[None], jnp.exp(log_g[:, None, None] * jnp.where(causal, diff, 0.0)[None]), 0.0)
    xi = jnp.exp(log_g[:, None] * (idx[None, :] + 1.0))
    zeta = jnp.exp(log_g[:, None] * (C - 1.0 - idx)[None, :])
    gc = jnp.exp(log_g * C)
    bc = lambda t: jnp.broadcast_to(t[:, :, None], (H_R, C, DR))
    return dmat, bc(xi), bc(zeta), jnp.broadcast_to(gc[:, None, None], (H_R, 1, DR))


def _rope_tables(pos):
    half = DR // 2
    inv = 1.0 / (ROPE_BASE ** (jnp.arange(half, dtype=F32) / half))
    ang = pos.astype(F32)[:, None] * inv[None, :]
    cos, sin = jnp.cos(ang), jnp.sin(ang)
    return jnp.concatenate([cos, cos], -1), jnp.concatenate([-sin, sin], -1)


def _retention(z, pos, r0, g_ret_l, n_batch, C, out_dtype, precise=False):
    T = z.shape[0]
    nc = T // (n_batch * C)
    cos, sin = _rope_tables(pos)
    dmat, xi, zeta, gc = _retention_consts(C)
    zspec = lambda col: pl.BlockSpec((C, W_R), lambda b, c: (b * nc + c, col))
    full = lambda shp: pl.BlockSpec(shp, lambda b, c: (0,) * len(shp))
    state_spec = pl.BlockSpec((None, H_R, DR, DR), lambda b, c: (b, 0, 0, 0))
    vmem = 2 * 5 * C * W_R * 4 + 2 * (H_R * C * C + 2 * H_R * C * DR) * 4 + 5 * H_R * DR * DR * 4 + (8 << 20)
    return pl.pallas_call(
        functools.partial(_retention_kernel, nc=nc, pr=precise),
        grid=(n_batch, nc),
        in_specs=[zspec(COL_RQ), zspec(COL_RK), zspec(COL_RV), zspec(COL_RG),
                  pl.BlockSpec((C, DR), lambda b, c: (c, 0)),
                  pl.BlockSpec((C, DR), lambda b, c: (c, 0)),
                  full((H_R, C, C)), full((H_R, C, DR)), full((H_R, C, DR)), full((H_R, 1, DR)),
                  full((H_R, 1, DR)), state_spec],
        out_specs=[pl.BlockSpec((C, W_R), lambda b, c: (b * nc + c, 0)), state_spec],
        out_shape=[jax.ShapeDtypeStruct((T, W_R), out_dtype),
                   jax.ShapeDtypeStruct((n_batch, H_R, DR, DR), F32)],
        scratch_shapes=[pltpu.VMEM((H_R, DR, DR), F32)],
        compiler_params=_cparams(2, vmem),
        name="retention",
    )(z, z, z, z, cos, sin, dmat, xi, zeta, gc, g_ret_l[:, None, :], r0)


def _stack_queries(q, qs_ref, n):
    del n
    lane = lax.broadcasted_iota(jnp.int32, q.shape, 1)
    q = q * (DQK ** -0.5 * LOG2E)
    _store_parts(qs_ref, jnp.concatenate([jnp.where(lane < DQK, q, 0.0), jnp.where(lane >= DQK, q, 0.0)], axis=0))


def _value_parts(v, precise):
    parts = _parts(v, precise)
    lane = lax.broadcasted_iota(jnp.int32, v.shape, 1)
    ones = jnp.where(lane == 0, 1.0, 0.0).astype(BF16)
    tails = (ones,) + (jnp.zeros(v.shape, BF16),) * (len(parts) - 1)
    return tuple(jnp.concatenate([p, t], axis=1) for p, t in zip(parts, tails))


def _softmax_update(score_blocks, value_blocks, m_ref, acc_ref, rows, precise=False):
    m_prev = m_ref[rows]
    m_new = m_prev
    for s in score_blocks:
        m_new = jnp.maximum(m_new, jnp.max(s, axis=-1, keepdims=True))
    pv = None
    for s, va in zip(score_blocks, value_blocks):
        t = _dot(_parts(jnp.exp2(s - m_new), precise), va)
        pv = t if pv is None else pv + t
    acc_ref[rows] = jnp.exp2(m_prev - m_new) * acc_ref[rows] + pv
    m_ref[rows] = m_new


def _diff_finalize(lamv, g, acc, n, lam_init):
    lam = (jnp.exp(jnp.sum(lamv[0:1] * lamv[1:2], keepdims=True))
           - jnp.exp(jnp.sum(lamv[2:3] * lamv[3:4], keepdims=True)) + lam_init)
    o1 = acc[0:n, 0:DVD] / acc[0:n, DVD:DVD + 1]
    o2 = acc[n:2 * n, 0:DVD] / acc[n:2 * n, DVD:DVD + 1]
    o = o1 - lam * o2
    ms = jnp.mean(o * o, axis=-1, keepdims=True)
    return o * lax.rsqrt(ms + EPS) * g * (1.0 - lam_init)


def _attn_init(q, qs, m_s, acc, n):
    _stack_queries(q, qs, n)
    m_s[...] = jnp.full(m_s.shape, NEG, F32)
    acc[...] = jnp.zeros(acc.shape, F32)


def _attn_prompt_kernel(it, jt, q_ref, k_ref, v_ref, lamv, gdn, o_ref, qs, m_s, acc, *, tq, rc, lam_init):
    t = pl.program_id(2)
    i = it[t]
    j = jt[t]

    @pl.when(j == 0)
    def _():
        _attn_init(q_ref[...], qs, m_s, acc, tq)

    def update(masked):
        kb = k_ref[...].astype(BF16)
        va = _value_parts(v_ref[...], False)
        for c in range(2 * tq // rc):
            rows = slice(c * rc, (c + 1) * rc)
            s = _mm_nt(qs[0, rows], kb)
            if masked:
                row = lax.broadcasted_iota(jnp.int32, s.shape, 0) + (c * rc) % tq
                col = lax.broadcasted_iota(jnp.int32, s.shape, 1)
                s = jnp.where(col <= row, s, NEG)
            _softmax_update([s], [va], m_s, acc, rows)

    @pl.when(j < i)
    def _():
        update(False)

    @pl.when(j == i)
    def _():
        update(True)
        o_ref[...] = _diff_finalize(lamv[...], gdn[...], acc, tq, lam_init).astype(o_ref.dtype)


def _attn_prompt(z, k, v, lamv, g_dn_l, n_batch, seq, lam_init):
    T = z.shape[0]
    tq = 512
    rc = 256
    nq = seq // tq
    cb = 1024 // DVD
    pairs = [(i, j) for i in range(nq) for j in range(i + 1)]
    it = jnp.asarray([p[0] for p in pairs], jnp.int32)
    jt = jnp.asarray([p[1] for p in pairs], jnp.int32)
    vmem = 2 * 3 * tq * DVD * 4 + 2 * tq * DVD * 2 + 2 * tq * DVD * 2 + 2 * tq * 2 * DVD * 4 \
        + 2 * tq * LANES * 4 + 8 * rc * tq * 4 + (8 << 20)
    return pl.pallas_call(
        functools.partial(_attn_prompt_kernel, tq=tq, rc=rc, lam_init=lam_init),
        grid_spec=pltpu.PrefetchScalarGridSpec(
            num_scalar_prefetch=2,
            grid=(n_batch, H_D, len(pairs)),
            in_specs=[
                pl.BlockSpec((tq, DVD), lambda b, h, t, it, jt: (b * nq + it[t], COL_DQ * cb + h)),
                pl.BlockSpec((tq, DVD), lambda b, h, t, it, jt: (b * nq + jt[t], h)),
                pl.BlockSpec((tq, DVD), lambda b, h, t, it, jt: (b * nq + jt[t], h)),
                pl.BlockSpec((4, DQK), lambda b, h, t, it, jt: (0, 0)),
                pl.BlockSpec((None, 1, DVD), lambda b, h, t, it, jt: (h, 0, 0)),
            ],
            out_specs=pl.BlockSpec((tq, DVD), lambda b, h, t, it, jt: (b * nq + it[t], h)),
            scratch_shapes=[pltpu.VMEM((1, 2 * tq, DVD), BF16), pltpu.VMEM((2 * tq, 1), F32),
                            pltpu.VMEM((2 * tq, 2 * DVD), F32)],
        ),
        out_shape=jax.ShapeDtypeStruct((T, W_D), BF16),
        compiler_params=_cparams(3, vmem),
        name="attn_prompt",
    )(it, jt, z, k, v, lamv, g_dn_l[:, None, :])


PAGES_PER_STEP = 8


def _attn_sample_kernel(pt_ref, q_ref, *rest, n_steps, lam_init):
    del pt_ref
    kp = rest[:PAGES_PER_STEP]
    vp = rest[PAGES_PER_STEP:2 * PAGES_PER_STEP]
    kn, vn, bias, bias_new, lamv, gdn, o_ref, qs, m_s, acc = rest[2 * PAGES_PER_STEP:]
    n = q_ref.shape[0]
    step = pl.program_id(1)
    every_row = slice(0, 2 * n)
    pr = qs.shape[0] == 2

    @pl.when(step == 0)
    def _():
        _attn_init(q_ref[...], qs, m_s, acc, n)

    rows = PAGE_SIZE * H_D
    q = _load_parts(qs)
    scores = [_dot(q, _parts(kp[p][...].reshape(rows, DVD), pr), _mm_nt) + bias[...] for p in range(PAGES_PER_STEP)]
    values = [_value_parts(vp[p][...].reshape(rows, DVD), pr) for p in range(PAGES_PER_STEP)]
    _softmax_update(scores, values, m_s, acc, every_row, pr)

    @pl.when(step == n_steps - 1)
    def _():
        s = _dot(q, _parts(kn[...], pr), _mm_nt) + bias_new[...]
        _softmax_update([s], [_value_parts(vn[...], pr)], m_s, acc, every_row, pr)
        o_ref[...] = _diff_finalize(lamv[...], gdn[...], acc, n, lam_init).astype(o_ref.dtype)


def _attn_sample(zs, ks, vs, cache_k, cache_v, page_table, l, lamv, g_dn_l, n_batch, n_tok, lam_init, precise=True):
    n_pages = page_table.shape[1]
    n_steps = n_pages // PAGES_PER_STEP
    n = n_tok * H_D
    rows = PAGE_SIZE * H_D
    q = zs[:, COL_DQ * 1024:(COL_DQ + 1) * 1024].reshape(n_batch, n, DVD)
    pad = lambda t: jnp.pad(t.reshape(n_batch, n, DVD), ((0, 0), (0, LANES - n), (0, 0)))
    kn = pad(ks)
    vn = pad(vs)
    r = jnp.arange(2 * n)
    c = jnp.arange(rows)
    bias = jnp.where((r % H_D)[:, None] == (c % H_D)[None, :], 0.0, NEG).astype(F32)
    cn = jnp.arange(LANES)
    ok = ((r % H_D)[:, None] == (cn % H_D)[None, :]) & ((cn // H_D)[None, :] <= ((r % n) // H_D)[:, None]) \
        & (cn < n)[None, :]
    bias_new = jnp.where(ok, 0.0, NEG).astype(F32)
    gdn = jnp.tile(g_dn_l, (n_tok, 1))

    def page_spec(p):
        return pl.BlockSpec((None, None, PAGE_SIZE, H_D, DVD),
                            lambda b, s, pt: (l, pt[b * n_pages + s * PAGES_PER_STEP + p], 0, 0, 0))

    const = lambda shp: pl.BlockSpec(shp, lambda b, s, pt: (0,) * len(shp))
    per_b = lambda shp: pl.BlockSpec((None,) + shp, lambda b, s, pt: (b, 0, 0))
    page_bytes = rows * DVD * 4
    n_p = _n_parts(precise)
    vmem = 2 * 2 * PAGES_PER_STEP * page_bytes + 2 * (2 * n) * rows * 4 \
        + (1 + n_p) * PAGES_PER_STEP * (2 * n) * rows * 4 + n_p * PAGES_PER_STEP * rows * 3 * DVD * 2 + (8 << 20)
    out = pl.pallas_call(
        functools.partial(_attn_sample_kernel, n_steps=n_steps, lam_init=lam_init),
        grid_spec=pltpu.PrefetchScalarGridSpec(
            num_scalar_prefetch=1,
            grid=(n_batch, n_steps),
            in_specs=[per_b((n, DVD))]
            + [page_spec(p) for p in range(PAGES_PER_STEP)]
            + [page_spec(p) for p in range(PAGES_PER_STEP)]
            + [per_b((LANES, DVD)), per_b((LANES, DVD)), const((2 * n, rows)), const((2 * n, LANES)),
               const((4, DQK)), const((n, DVD))],
            out_specs=per_b((n, DVD)),
            scratch_shapes=[pltpu.VMEM((n_p, 2 * n, DVD), BF16), pltpu.VMEM((2 * n, 1), F32),
                            pltpu.VMEM((2 * n, 2 * DVD), F32)],
        ),
        out_shape=jax.ShapeDtypeStruct((n_batch, n, DVD), F32 if precise else BF16),
        compiler_params=_cparams(2, vmem),
        name="attn_sample",
    )(page_table.reshape(-1), q, *([cache_k] * PAGES_PER_STEP), *([cache_v] * PAGES_PER_STEP),
      kn, vn, bias, bias_new, lamv, gdn)
    return out.reshape(n_batch * n_tok, W_D)


def _mix_kernel(yr, yd, ga, gb, wpa, wpb, o_ref, *, pr):
    a = _dot(_parts(yr[...], pr), _parts(wpa[...], pr))
    d = _dot(_parts(yd[...], pr), _parts(wpb[...], pr))
    o_ref[...] = (jax.nn.sigmoid(ga[...]) * a + jax.nn.sigmoid(gb[...]) * d).astype(o_ref.dtype)


def _mix(yr, yd, z, w_pa, w_pb, l, tm, precise=False):
    T = yr.shape[0]
    tn = 1024
    vmem = 2 * 2 * tm * W_R * 4 + 2 * 2 * tm * tn * 4 + 2 * 2 * W_R * tn * 4 + 4 * W_R * tn * 2 \
        + 2 * tm * tn * 4 + 4 * tm * tn * 4 + (6 << 20)
    return pl.pallas_call(
        functools.partial(_mix_kernel, pr=precise),
        grid=(T // tm, D_MODEL // tn),
        in_specs=[
            pl.BlockSpec((tm, W_R), lambda m, n: (m, 0)),
            pl.BlockSpec((tm, W_D), lambda m, n: (m, 0)),
            pl.BlockSpec((tm, tn), lambda m, n: (m, COL_GA + n)),
            pl.BlockSpec((tm, tn), lambda m, n: (m, COL_GB + n)),
            pl.BlockSpec((None, W_R, tn), lambda m, n: (l, 0, n)),
            pl.BlockSpec((None, W_D, tn), lambda m, n: (l, 0, n)),
        ],
        out_specs=pl.BlockSpec((tm, tn), lambda m, n: (m, n)),
        out_shape=jax.ShapeDtypeStruct((T, D_MODEL), F32 if precise else BF16),
        compiler_params=_cparams(2, vmem),
        name="mix",
    )(yr, yd, z, z, w_pa, w_pb)


def _outproj_kernel(mix, wo, x_ref, gt, g2, sc, sh, x1_ref, h2_ref):
    pr = wo.dtype == F32
    x1 = x_ref[...] + gt[...] * _dot(_parts(mix[...], pr), _parts(wo[...], pr))
    x1_ref[...] = x1
    ms = jnp.mean(x1 * x1, axis=-1, keepdims=True)
    h = x1 * lax.rsqrt(ms + EPS) * g2[...]
    h = (h * (1.0 + sc[...]) + sh[...]).astype(h2_ref.dtype)
    if len(h2_ref.shape) == 3:
        _store_chunked(h2_ref, h)
    else:
        h2_ref[...] = h


def _outproj(mix, wo, l, x, mods, g2, tm, chunked_h2):
    T = x.shape[0]
    precise = wo.dtype == F32
    row = lambda: pl.BlockSpec((tm, D_MODEL), lambda m: (m, 0))
    if chunked_h2:
        h2_spec = pl.BlockSpec((tm, ROW_CHUNKS, LANES), lambda m: (m, 0, 0))
        h2_shape = jax.ShapeDtypeStruct((T, ROW_CHUNKS, LANES), F32)
    else:
        h2_spec, h2_shape = row(), jax.ShapeDtypeStruct((T, D_MODEL), F32 if precise else BF16)
    vmem = 2 * tm * D_MODEL * (4 + 4 + 4 + 4) + 2 * D_MODEL * D_MODEL * wo.dtype.itemsize \
        + (2 * D_MODEL * D_MODEL * 2 if precise else 0) + 4 * tm * D_MODEL * 4 + (6 << 20)
    return pl.pallas_call(
        _outproj_kernel,
        grid=(T // tm,),
        in_specs=[row(), pl.BlockSpec((None, D_MODEL, D_MODEL), lambda m: (l, 0, 0)), row(),
                  mods.spec(2, tm), pl.BlockSpec((1, D_MODEL), lambda m: (0, 0)),
                  mods.spec(4, tm), mods.spec(3, tm)],
        out_specs=[row(), h2_spec],
        out_shape=[jax.ShapeDtypeStruct((T, D_MODEL), F32), h2_shape],
        compiler_params=_cparams(1, vmem),
        name="outproj",
    )(mix, wo, x, mods.arr, g2, mods.arr, mods.arr)


def _ffn_up_kernel(te, first, nv, xs, w1, w3, o_ref, w1b, w3b):
    del te
    m = pl.program_id(1)

    @pl.when(first[m] == 1)
    def _():
        _store_parts(w1b, w1[...])
        _store_parts(w3b, w3[...])

    @pl.when(m < nv[0])
    def _():
        x = _parts(xs[...], w1b.shape[0] == 2)
        o_ref[...] = (_silu(_dot(x, _load_parts(w1b))) * _dot(x, _load_parts(w3b))).astype(o_ref.dtype)

    @pl.when(m >= nv[0])
    def _():
        o_ref[...] = jnp.zeros(o_ref.shape, o_ref.dtype)


def _ffn_up(xs, w1, w3, te, first, nv, tm, tf, precise=False):
    P = xs.shape[0]
    ff = w1.shape[-1]
    n_mt = pl.cdiv(P, tm)
    xb = xs.dtype.itemsize
    n_p = _n_parts(precise)
    vmem = 2 * tm * D_MODEL * xb + 2 * 2 * D_MODEL * tf * 4 + 2 * n_p * D_MODEL * tf * 2 + 2 * tm * tf * 4 \
        + 4 * tm * tf * 4 + n_p * tm * D_MODEL * 2 + (6 << 20)
    return pl.pallas_call(
        _ffn_up_kernel,
        grid_spec=pltpu.PrefetchScalarGridSpec(
            num_scalar_prefetch=3,
            grid=(pl.cdiv(ff, tf), n_mt),
            in_specs=[
                pl.BlockSpec((tm, D_MODEL), lambda f, m, te, fi, nv: (m, 0)),
                pl.BlockSpec((None, D_MODEL, tf), lambda f, m, te, fi, nv: (te[m], 0, f)),
                pl.BlockSpec((None, D_MODEL, tf), lambda f, m, te, fi, nv: (te[m], 0, f)),
            ],
            out_specs=pl.BlockSpec((tm, tf), lambda f, m, te, fi, nv: (m, f)),
            scratch_shapes=[pltpu.VMEM((n_p, D_MODEL, tf), BF16), pltpu.VMEM((n_p, D_MODEL, tf), BF16)],
        ),
        out_shape=jax.ShapeDtypeStruct((P, ff), F32 if precise else BF16),
        compiler_params=_cparams(2, vmem),
        name="ffn_up",
    )(te, first, nv, xs, w1, w3)


def _ffn_down_kernel(te, first, nv, act, w2, o_ref, w2b):
    del te
    m = pl.program_id(1)

    @pl.when(first[m] == 1)
    def _():
        _store_parts(w2b, w2[...])

    @pl.when(m < nv[0])
    def _():
        o_ref[...] = _dot(_parts(act[...], w2b.shape[0] == 2), _load_parts(w2b))

    @pl.when(m >= nv[0])
    def _():
        o_ref[...] = jnp.zeros(o_ref.shape, o_ref.dtype)


def _ffn_down(act, w2, te, first, nv, tm, tn, precise=False):
    P, ff = act.shape
    n_mt = pl.cdiv(P, tm)
    n_p = _n_parts(precise)
    vmem = (2 + n_p) * tm * ff * act.dtype.itemsize + 2 * ff * tn * 4 + n_p * ff * tn * 2 + 4 * tm * tn * 4 + (4 << 20)
    return pl.pallas_call(
        _ffn_down_kernel,
        grid_spec=pltpu.PrefetchScalarGridSpec(
            num_scalar_prefetch=3,
            grid=(D_MODEL // tn, n_mt),
            in_specs=[
                pl.BlockSpec((tm, ff), lambda n, m, te, fi, nv: (m, 0)),
                pl.BlockSpec((None, ff, tn), lambda n, m, te, fi, nv: (te[m], 0, n)),
            ],
            out_specs=pl.BlockSpec((tm, tn), lambda n, m, te, fi, nv: (m, n)),
            scratch_shapes=[pltpu.VMEM((n_p, ff, tn), BF16)],
        ),
        out_shape=jax.ShapeDtypeStruct((P, D_MODEL), F32),
        compiler_params=_cparams(2, vmem),
        name="ffn_down",
    )(te, first, nv, act, w2)


def _single_expert_tiles(P, tm, e):
    n_mt = pl.cdiv(P, tm)
    te = jnp.full((n_mt,), e, jnp.int32)
    first = jnp.zeros((n_mt,), jnp.int32).at[0].set(1)
    return te, first, jnp.full((1,), n_mt, jnp.int32)


def _residual_kernel(x1, gt, y, o_ref):
    o_ref[...] = x1[...] + gt[...] * y[...]


def _residual(x1, mods, y, tm):
    T = x1.shape[0]
    row = lambda: pl.BlockSpec((tm, D_MODEL), lambda m: (m, 0))
    return pl.pallas_call(
        _residual_kernel,
        grid=(T // tm,),
        in_specs=[row(), mods.spec(5, tm), row()],
        out_specs=row(),
        out_shape=jax.ShapeDtypeStruct((T, D_MODEL), F32),
        compiler_params=_cparams(1, 8 * tm * D_MODEL * 4 + (4 << 20)),
        name="residual",
    )(x1, mods.arr, y)


def _router_kernel(h_ref, wr_ref, o_ref):
    logits = None
    for c in range(ROW_CHUNKS):
        part = jnp.dot(h_ref[:, c, :], wr_ref[c * LANES:(c + 1) * LANES, :], preferred_element_type=F32,
                       precision=lax.Precision.HIGHEST)
        logits = part if logits is None else logits + part
    lane = lax.broadcasted_iota(jnp.int32, logits.shape, 1)
    logits = jnp.where(lane < N_EXPERTS, logits, NEG)
    v1 = jnp.max(logits, axis=-1, keepdims=True)
    i1 = jnp.min(jnp.where(logits == v1, lane, LANES), axis=-1, keepdims=True)
    rest = jnp.where(lane == i1, NEG, logits)
    v2 = jnp.max(rest, axis=-1, keepdims=True)
    i2 = jnp.min(jnp.where(rest == v2, lane, LANES), axis=-1, keepdims=True)
    e2 = jnp.exp(v2 - v1)
    w1 = 1.0 / (1.0 + e2)
    w2 = e2 / (1.0 + e2)
    o_ref[...] = jnp.where(lane == 0, i1.astype(F32),
                           jnp.where(lane == 1, i2.astype(F32),
                                     jnp.where(lane == 2, w1, jnp.where(lane == 3, w2, 0.0))))


def _router(h2, w_router_l, tm):
    T = h2.shape[0]
    wr = jnp.pad(w_router_l, ((0, 0), (0, LANES - N_EXPERTS)))
    return pl.pallas_call(
        _router_kernel,
        grid=(pl.cdiv(T, tm),),
        in_specs=[pl.BlockSpec((tm, ROW_CHUNKS, LANES), lambda m: (m, 0, 0)),
                  pl.BlockSpec((D_MODEL, LANES), lambda m: (0, 0))],
        out_specs=pl.BlockSpec((tm, LANES), lambda m: (m, 0)),
        out_shape=jax.ShapeDtypeStruct((T, LANES), F32),
        compiler_params=_cparams(1, 6 * tm * D_MODEL * 4 + (8 << 20)),
        name="router",
    )(h2, wr)


DMA_UNROLL = 8


def _row_copy(src_hbm, row, dst_vmem, slot, sem):
    return pltpu.make_async_copy(src_hbm.at[pl.ds(row, 1)], dst_vmem.at[pl.ds(slot, 1)], sem)


def _dispatch_kernel(src, h_hbm, o_ref, buf, sem, *, tm):
    base = pl.program_id(0) * tm

    def start(r, carry):
        _row_copy(h_hbm, src[base + r], buf, r, sem).start()
        return carry

    def wait(r, carry):
        _row_copy(h_hbm, 0, buf, r, sem).wait()
        return carry

    lax.fori_loop(0, tm, start, 0, unroll=DMA_UNROLL)
    lax.fori_loop(0, tm, wait, 0, unroll=DMA_UNROLL)
    for c in range(ROW_CHUNKS):
        o_ref[:, c * LANES:(c + 1) * LANES] = buf[:, c, :].astype(o_ref.dtype)


def _dispatch(h2, src, tm, out_dtype):
    P = src.shape[0]
    return pl.pallas_call(
        functools.partial(_dispatch_kernel, tm=tm),
        grid_spec=pltpu.PrefetchScalarGridSpec(
            num_scalar_prefetch=1,
            grid=(P // tm,),
            in_specs=[pl.BlockSpec(memory_space=pl.ANY)],
            out_specs=pl.BlockSpec((tm, D_MODEL), lambda m, src: (m, 0)),
            scratch_shapes=[pltpu.VMEM((tm, ROW_CHUNKS, LANES), F32), pltpu.SemaphoreType.DMA(())],
        ),
        out_shape=jax.ShapeDtypeStruct((P, D_MODEL), out_dtype),
        compiler_params=_cparams(1, 2 * tm * D_MODEL * 4 + 3 * tm * D_MODEL * 4 + (4 << 20)),
        name="dispatch",
    )(src, h2)


def _combine_kernel(pos, y_hbm, rout, x1, gt, o_ref, buf, sem, *, tm):
    base = pl.program_id(0) * tm

    def start(r, carry):
        for s in range(TOP_K):
            _row_copy(y_hbm, pos[(base + r) * TOP_K + s], buf.at[s], r, sem).start()
        return carry

    def wait(r, carry):
        for s in range(TOP_K):
            _row_copy(y_hbm, 0, buf.at[s], r, sem).wait()
        return carry

    lax.fori_loop(0, tm, start, 0, unroll=DMA_UNROLL)
    lax.fori_loop(0, tm, wait, 0, unroll=DMA_UNROLL)
    g = rout[...]
    ffn = g[:, 2:3] * buf[0] + g[:, 3:4] * buf[1]
    o_ref[...] = x1[...] + gt[...] * ffn


def _combine(y, pos, rout, x1, mods, tm):
    T = x1.shape[0]
    row = lambda: pl.BlockSpec((tm, D_MODEL), lambda m, pos: (m, 0))
    return pl.pallas_call(
        functools.partial(_combine_kernel, tm=tm),
        grid_spec=pltpu.PrefetchScalarGridSpec(
            num_scalar_prefetch=1,
            grid=(T // tm,),
            in_specs=[pl.BlockSpec(memory_space=pl.ANY),
                      pl.BlockSpec((tm, LANES), lambda m, pos: (m, 0)),
                      row(), mods.spec(5, tm)],
            out_specs=row(),
            scratch_shapes=[pltpu.VMEM((TOP_K, tm, D_MODEL), F32), pltpu.SemaphoreType.DMA(())],
        ),
        out_shape=jax.ShapeDtypeStruct((T, D_MODEL), F32),
        compiler_params=_cparams(1, (TOP_K + 8) * tm * D_MODEL * 4 + (4 << 20)),
        name="combine",
    )(pos, y, rout, x1, mods.arr)


def _route_tables(rout, tm, n_mt):
    n_tok = rout.shape[0]
    ef = rout[:, :TOP_K].astype(jnp.int32).reshape(-1)
    onehot = (ef[:, None] == jnp.arange(N_EXPERTS)[None, :]).astype(jnp.int32)
    counts = jnp.sum(onehot, axis=0)
    padded = ((counts + tm - 1) // tm) * tm
    ends = jnp.cumsum(padded)
    offs = ends - padded
    rank = jnp.sum((jnp.cumsum(onehot, axis=0) - onehot) * onehot, axis=1)
    pos = (offs[ef] + rank).astype(jnp.int32)
    src = jnp.zeros((n_mt * tm,), jnp.int32).at[pos].set(jnp.arange(n_tok * TOP_K, dtype=jnp.int32) // TOP_K)
    n_valid = (ends[-1] // tm).astype(jnp.int32)
    starts = jnp.arange(n_mt, dtype=jnp.int32) * tm
    te = jnp.minimum(jnp.searchsorted(ends, starts, side="right"), N_EXPERTS - 1).astype(jnp.int32)
    te = jnp.where(jnp.arange(n_mt) < n_valid, te, te[jnp.maximum(n_valid - 1, 0)])
    first = jnp.concatenate([jnp.ones((1,), jnp.int32), (te[1:] != te[:-1]).astype(jnp.int32)])
    return pos, src, te, first, n_valid.reshape(1)


def kernel(x_prompt, x_sample, cache_k, cache_v, state_ret, page_table, c_prompt, c_sample, w_ada, b_ada,
           g_norm1, g_norm2, w_in, g_qn, g_kn, lam_q1, lam_k1, lam_q2, lam_k2, g_ret, g_dn, w_pa, w_pb, w_o,
           w_ff1, w_ff3, w_ff2, w_router, w_e1, w_e3, w_e2):
    B, S, _ = x_prompt.shape
    Bd, Sd, _ = x_sample.shape
    n_pages = page_table.shape[1]
    past = n_pages * cache_k.shape[2]
    Tp, Ts = B * S, Bd * Sd
    tm_p = 512

    c_all = jnp.concatenate([c_prompt, c_sample], axis=0)
    c_all = jnp.pad(c_all, ((0, (-c_all.shape[0]) % 8), (0, 0)))
    mod = _adaln(c_all, w_ada, b_ada).reshape(DEPTH, c_all.shape[0], 6, D_MODEL)

    pos_p = jnp.arange(S)
    pos_s = past + jnp.arange(Sd)
    r0_p = jnp.zeros((B, H_R, DR, DR), F32)
    n_exp_w = w_e1.shape[0] * N_EXPERTS
    we1 = w_e1.reshape(n_exp_w, D_MODEL, -1)
    we3 = w_e3.reshape(n_exp_w, D_MODEL, -1)
    we2 = w_e2.reshape(n_exp_w, -1, D_MODEL)

    xp = x_prompt.reshape(Tp, D_MODEL)
    xs = x_sample.reshape(Ts, D_MODEL)
    kp, vp, rp, ksl, vsl, rsl = [], [], [], [], [], []
    for l in range(DEPTH):
        lam_init = 0.8 - 0.6 * math.exp(-0.3 * l)
        lamv = jnp.stack([lam_q1[l], lam_k1[l], lam_q2[l], lam_k2[l]]).astype(F32)
        mods_p = _Mods(mod[l, :B], "prompt", S)
        mods_s = _Mods(mod[l, B:B + Bd], "sample", Sd)
        wo_b = w_o[l].astype(BF16)[None]
        g2 = g_norm2[l][None]
        moe = l % 2 == 1

        zp, k_p, v_p = _inproj(xp, mods_p, l, g_norm1, w_in, g_qn, g_kn, 1024)
        yr_p, r_new_p = _retention(zp, pos_p, r0_p, g_ret[l], B, RET_CHUNK, BF16)
        yd_p = _attn_prompt(zp, k_p, v_p, lamv, g_dn[l], B, S, lam_init)
        mix_p = _mix(yr_p, yd_p, zp, w_pa, w_pb, l, tm_p)
        x1_p, h2_p = _outproj(mix_p, wo_b, 0, xp, mods_p, g2, tm_p, moe)
        kp.append(k_p.reshape(B, S, H_D, 2 * DQK))
        vp.append(v_p.reshape(B, S, H_D, DVD))
        rp.append(r_new_p)

        zs, k_s, v_s = _inproj(xs, mods_s, l, g_norm1, w_in, g_qn, g_kn, Ts, precise=True)
        yr_s, r_new_s = _retention(zs, pos_s, state_ret[l], g_ret[l], Bd, Sd, F32, precise=True)
        yd_s = _attn_sample(zs, k_s, v_s, cache_k, cache_v, page_table, l, lamv, g_dn[l], Bd, Sd, lam_init)
        mix_s = _mix(yr_s, yd_s, zs, w_pa, w_pb, l, Ts, precise=True)
        x1_s, h2_s = _outproj(mix_s, w_o, l, xs, mods_s, g2, Ts, moe)
        ksl.append(k_s.reshape(Bd, Sd, H_D, 2 * DQK))
        vsl.append(v_s.reshape(Bd, Sd, H_D, DVD))
        rsl.append(r_new_s)

        i = l // 2
        if not moe:
            outs = []
            for h2, x1, mods, tm, pr in ((h2_p, x1_p, mods_p, tm_p, False), (h2_s, x1_s, mods_s, Ts, True)):
                te, first, nv = _single_expert_tiles(h2.shape[0], tm, i)
                act = _ffn_up(h2, w_ff1, w_ff3, te, first, nv, tm, 512, pr)
                y = _ffn_down(act, w_ff2, te, first, nv, tm, 512, pr)
                outs.append(_residual(x1, mods, y, tm))
            xp, xs = outs
        else:
            def routed(h2, tm_e, pr):
                n_tok = h2.shape[0]
                n_mt = (n_tok * TOP_K + N_EXPERTS * (tm_e - 1)) // tm_e + 1
                rout = _router(h2, w_router[i], min(n_tok, 512))
                pos, src, te, first, nv = _route_tables(rout, tm_e, n_mt)
                te = te + i * N_EXPERTS
                xs_sorted = _dispatch(h2, src, tm_e, F32 if pr else BF16)
                act = _ffn_up(xs_sorted, we1, we3, te, first, nv, tm_e, 512, pr)
                return _ffn_down(act, we2, te, first, nv, tm_e, 512, pr), pos, rout

            if any(later % 2 == 1 for later in range(l + 1, DEPTH)):
                y, pos, rout = routed(h2_p, 512, False)
                xp = _combine(y, pos, rout, x1_p, mods_p, 256)
                y, pos, rout = routed(h2_s, Ts, True)
                xs = _combine(y, pos, rout, x1_s, mods_s, Ts)
            else:
                y, pos, rout = routed(jnp.concatenate([h2_p, h2_s], axis=0), 512, False)
                xp = _combine(y, pos[:Tp * TOP_K], rout[:Tp], x1_p, mods_p, 256)
                xs = _combine(y, pos[Tp * TOP_K:], rout[Tp:], x1_s, mods_s, Ts)

    return (xp.reshape(B, S, D_MODEL), xs.reshape(Bd, Sd, D_MODEL),
            jnp.stack(kp), jnp.stack(vp), jnp.stack(rp), jnp.stack(ksl), jnp.stack(vsl), jnp.stack(rsl))
```

```python
import functools
import math

import jax
import jax.numpy as jnp
from jax import lax
from jax.experimental import pallas as pl
from jax.experimental.pallas import tpu as pltpu

F32 = jnp.float32
BF16 = jnp.bfloat16

D_MODEL = 2048
DEPTH = 4
PAGE_SIZE = 128
H_R = 8
DR = 128
W_R = H_R * DR
RET_CHUNK = 128
ROPE_BASE = 10000.0
H_D = 8
DQK = 64
DVD = 2 * DQK
W_DQ = H_D * 2 * DQK
W_D = H_D * DVD
N_IN = 4 * W_R + 2 * W_DQ + W_D + 2 * D_MODEL
N_EXPERTS = 8
TOP_K = 2
EPS = 1e-6
NEG = -1e30
LOG2E = math.log2(math.e)

LANES = 128
VMEM_CAP_BYTES = 60000 * 1024

COL_RQ, COL_RK, COL_RV, COL_RG, COL_DQ, COL_DK, COL_DV, COL_GA, COL_GB = 0, 1, 2, 3, 4, 5, 6, 7, 9


def _cparams(n_axes, vmem_bytes):
    return pltpu.CompilerParams(
        dimension_semantics=("arbitrary",) * n_axes,
        vmem_limit_bytes=min(int(vmem_bytes), VMEM_CAP_BYTES))


def _mm(a, b):
    return jnp.dot(a, b, preferred_element_type=F32)


def _mm_nt(a, b):
    return lax.dot_general(a, b, (((1,), (1,)), ((), ())), preferred_element_type=F32)


def _mm_tn(a, b):
    return lax.dot_general(a, b, (((0,), (0,)), ((), ())), preferred_element_type=F32)


def _silu(x):
    return x * jax.nn.sigmoid(x)


def _parts(x, precise):
    hi = x.astype(BF16)
    if not precise or x.dtype == BF16:
        return (hi,)
    return (hi, (x - hi.astype(F32)).astype(BF16))


def _dot(a_parts, b_parts, mm=_mm):
    rows = a_parts[0].shape[0]
    if len(a_parts) == 2 and mm is not _mm_tn and rows % 16 == 0:
        both = mm(jnp.concatenate(a_parts, axis=0), b_parts[0])
        out = both[:rows] + both[rows:]
        return out + mm(a_parts[0], b_parts[1]) if len(b_parts) == 2 else out
    out = None
    for i, a in enumerate(a_parts):
        for j, b in enumerate(b_parts):
            if i + j < 2:
                t = mm(a, b)
                out = t if out is None else out + t
    return out


def _n_parts(precise):
    return 2 if precise else 1


def _load_parts(ref):
    return tuple(ref[p] for p in range(ref.shape[0]))


def _store_parts(ref, x):
    for p, part in enumerate(_parts(x, ref.shape[0] == 2)):
        ref[p] = part


ROW_CHUNKS = D_MODEL // LANES


def _store_chunked(ref3, value):
    for c in range(value.shape[1] // LANES):
        ref3[:, c, :] = value[:, c * LANES:(c + 1) * LANES]


def _adaln_kernel(c_ref, w_ref, b_ref, o_ref):
    o_ref[...] = _dot(_parts(_silu(c_ref[...]), True), _parts(w_ref[...], True)) + b_ref[...]


def _adaln(c_all, w_ada, b_ada):
    rows = c_all.shape[0]
    tn = 1536
    n_out = 6 * D_MODEL
    return pl.pallas_call(
        _adaln_kernel,
        grid=(DEPTH, n_out // tn),
        in_specs=[
            pl.BlockSpec((rows, D_MODEL), lambda l, n: (0, 0)),
            pl.BlockSpec((None, D_MODEL, tn), lambda l, n: (l, 0, n)),
            pl.BlockSpec((None, 1, tn), lambda l, n: (l, 0, n)),
        ],
        out_specs=pl.BlockSpec((None, rows, tn), lambda l, n: (l, 0, n)),
        out_shape=jax.ShapeDtypeStruct((DEPTH, rows, n_out), F32),
        compiler_params=_cparams(2, 2 * D_MODEL * tn * 4 + 4 * D_MODEL * tn * 2 + (8 << 20)),
        name="adaln",
    )(c_all, w_ada, b_ada.reshape(DEPTH, 1, n_out))


class _Mods:
    def __init__(self, mod_l, group, rows_per_batch=None):
        self.group = group
        if group == "prompt":
            self.arr = mod_l.reshape(mod_l.shape[0], 6, 1, D_MODEL)
            self.rows_per_batch = rows_per_batch
        else:
            n_rep = rows_per_batch
            self.arr = jnp.transpose(jnp.repeat(mod_l, n_rep, axis=0), (1, 0, 2))

    def spec(self, j, tm):
        if self.group == "prompt":
            per = self.rows_per_batch // tm
            return pl.BlockSpec((None, None, 1, D_MODEL), lambda *g: (g[0] // per, j, 0, 0))
        return pl.BlockSpec((None, self.arr.shape[1], D_MODEL), lambda *g: (j, 0, 0))


def _inproj_kernel(x_ref, g_ref, sc_ref, sh_ref, w_ref, gq_ref, gk_ref, bd_ref, o_ref, k_ref, v_ref, h_scr, *, tn):
    n = pl.program_id(1)
    precise = h_scr.shape[0] == 2

    @pl.when(n == 0)
    def _():
        x = x_ref[...]
        ms = jnp.mean(x * x, axis=-1, keepdims=True)
        h = x * lax.rsqrt(ms + EPS) * g_ref[...]
        _store_parts(h_scr, h * (1.0 + sc_ref[...]) + sh_ref[...])

    z = _dot(_load_parts(h_scr), _parts(w_ref[...], precise))
    per = 1024 // tn
    is_q = (n >= COL_DQ * per) & (n < (COL_DQ + 1) * per)
    is_k = (n >= COL_DK * per) & (n < (COL_DK + 1) * per)
    is_v = (n >= COL_DV * per) & (n < (COL_DV + 1) * per)
    normed = is_q | is_k

    @pl.when(normed)
    def _():
        ms = _dot(_parts(z * z, precise), (bd_ref[...],))
        zn = z * lax.rsqrt(ms + EPS) * jnp.where(is_q, gq_ref[...], gk_ref[...])
        o_ref[...] = zn

        @pl.when(is_k)
        def _():
            k_ref[...] = zn

    @pl.when(jnp.logical_not(normed))
    def _():
        o_ref[...] = z

    @pl.when(is_v)
    def _():
        v_ref[...] = z


def _inproj(x, mods, l, g_norm1, w_in, g_qn, g_kn, tm, precise=False):
    T = x.shape[0]
    tn = 512
    per = 1024 // tn
    grp = jnp.arange(tn) // DQK
    bd = jnp.where(grp[:, None] == grp[None, :], 1.0 / DQK, 0.0).astype(BF16)
    gq = jnp.tile(g_qn[l], tn // DQK)[None]
    gk = jnp.tile(g_kn[l], tn // DQK)[None]

    def kv_spec(col):
        return pl.BlockSpec((tm, tn), lambda m, n: (m, jnp.clip(n - col * per, 0, per - 1)))

    n_p = _n_parts(precise)
    vmem = (2 * tm * D_MODEL * 4 + 2 * D_MODEL * tn * 4 + 6 * tm * tn * 4 + n_p * tm * D_MODEL * 2
            + (2 * n_p - 1) * D_MODEL * tn * 2 + 3 * tm * tn * 4 + 2 * tn * tn * 2 + (6 << 20))
    return pl.pallas_call(
        functools.partial(_inproj_kernel, tn=tn),
        grid=(T // tm, N_IN // tn),
        in_specs=[
            pl.BlockSpec((tm, D_MODEL), lambda m, n: (m, 0)),
            pl.BlockSpec((1, D_MODEL), lambda m, n: (0, 0)),
            mods.spec(1, tm),
            mods.spec(0, tm),
            pl.BlockSpec((None, D_MODEL, tn), lambda m, n: (l, 0, n)),
            pl.BlockSpec((1, tn), lambda m, n: (0, 0)),
            pl.BlockSpec((1, tn), lambda m, n: (0, 0)),
            pl.BlockSpec((tn, tn), lambda m, n: (0, 0)),
        ],
        out_specs=[pl.BlockSpec((tm, tn), lambda m, n: (m, n)), kv_spec(COL_DK), kv_spec(COL_DV)],
        out_shape=[jax.ShapeDtypeStruct((T, N_IN), F32), jax.ShapeDtypeStruct((T, W_DQ), F32),
                   jax.ShapeDtypeStruct((T, W_D), F32)],
        scratch_shapes=[pltpu.VMEM((n_p, tm, D_MODEL), BF16)],
        compiler_params=_cparams(2, vmem),
        name="inproj",
    )(x, g_norm1[l][None], mods.arr, mods.arr, w_in, gq, gk, bd)


def _retention_kernel(zq, zk, zv, zg, cos_ref, sin_ref, dmat, xi, zeta, gc, gr, r0, yr, rout, r_scr, *, nc, pr):
    c = pl.program_id(1)

    @pl.when(c == 0)
    def _():
        r_scr[...] = r0[...]

    cs = cos_ref[...]
    sn = sin_ref[...]
    for h in range(H_R):
        sl = slice(h * DR, (h + 1) * DR)
        q = zq[:, sl]
        k = zk[:, sl]
        q = q * cs + pltpu.roll(q, DR // 2, 1) * sn
        k = (k * cs + pltpu.roll(k, DR // 2, 1) * sn) * (DR ** -0.5)
        vb = _parts(zv[:, sl], pr)
        s = _dot(_parts(q, pr), _parts(k, pr), _mm_nt) * dmat[h]
        r_h = r_scr[h]
        o = _dot(_parts(s, pr), vb) + _dot(_parts(q * xi[h], pr), _parts(r_h, pr))
        r_scr[h] = gc[h] * r_h + _dot(_parts(k * zeta[h], pr), vb, _mm_tn)
        oc = o - jnp.mean(o, axis=-1, keepdims=True)
        y = oc * lax.rsqrt(jnp.mean(oc * oc, axis=-1, keepdims=True) + EPS) * gr[h]
        yr[:, sl] = (y * _silu(zg[:, sl])).astype(yr.dtype)

    @pl.when(c == nc - 1)
    def _():
        rout[...] = r_scr[...]


def _retention_consts(C):
    log_g = jnp.log1p(-jnp.exp2(-5.0 - jnp.arange(H_R, dtype=F32)))
    idx = jnp.arange(C, dtype=F32)
    diff = idx[:, None] - idx[None, :]
    causal = diff >= 0
    dmat = jnp.where(causal[None], jnp.exp(log_g[:, None, None] * jnp.where(causal, diff, 0.0)[None]), 0.0)
    xi = jnp.exp(log_g[:, None] * (idx[None, :] + 1.0))
    zeta = jnp.exp(log_g[:, None] * (C - 1.0 - idx)[None, :])
    gc = jnp.exp(log_g * C)
    bc = lambda t: jnp.broadcast_to(t[:, :, None], (H_R, C, DR))
    return dmat, bc(xi), bc(zeta), jnp.broadcast_to(gc[:, None, None], (H_R, 1, DR))


def _rope_tables(pos):
    half = DR // 2
    inv = 1.0 / (ROPE_BASE ** (jnp.arange(half, dtype=F32) / half))
    ang = pos.astype(F32)[:, None] * inv[None, :]
    cos, sin = jnp.cos(ang), jnp.sin(ang)
    return jnp.concatenate([cos, cos], -1), jnp.concatenate([-sin, sin], -1)


def _retention(z, pos, r0, g_ret_l, n_batch, C, out_dtype, precise=False):
    T = z.shape[0]
    nc = T // (n_batch * C)
    cos, sin = _rope_tables(pos)
    dmat, xi, zeta, gc = _retention_consts(C)
    zspec = lambda col: pl.BlockSpec((C, W_R), lambda b, c: (b * nc + c, col))
    full = lambda shp: pl.BlockSpec(shp, lambda b, c: (0,) * len(shp))
    state_spec = pl.BlockSpec((None, H_R, DR, DR), lambda b, c: (b, 0, 0, 0))
    vmem = 2 * 5 * C * W_R * 4 + 2 * (H_R * C * C + 2 * H_R * C * DR) * 4 + 5 * H_R * DR * DR * 4 + (8 << 20)
    return pl.pallas_call(
        functools.partial(_retention_kernel, nc=nc, pr=precise),
        grid=(n_batch, nc),
        in_specs=[zspec(COL_RQ), zspec(COL_RK), zspec(COL_RV), zspec(COL_RG),
                  pl.BlockSpec((C, DR), lambda b, c: (c, 0)),
                  pl.BlockSpec((C, DR), lambda b, c: (c, 0)),
                  full((H_R, C, C)), full((H_R, C, DR)), full((H_R, C, DR)), full((H_R, 1, DR)),
                  full((H_R, 1, DR)), state_spec],
        out_specs=[pl.BlockSpec((C, W_R), lambda b, c: (b * nc + c, 0)), state_spec],
        out_shape=[jax.ShapeDtypeStruct((T, W_R), out_dtype),
                   jax.ShapeDtypeStruct((n_batch, H_R, DR, DR), F32)],
        scratch_shapes=[pltpu.VMEM((H_R, DR, DR), F32)],
        compiler_params=_cparams(2, vmem),
        name="retention",
    )(z, z, z, z, cos, sin, dmat, xi, zeta, gc, g_ret_l[:, None, :], r0)


def _stack_queries(q, qs_ref, n):
    del n
    lane = lax.broadcasted_iota(jnp.int32, q.shape, 1)
    q = q * (DQK ** -0.5 * LOG2E)
    _store_parts(qs_ref, jnp.concatenate([jnp.where(lane < DQK, q, 0.0), jnp.where(lane >= DQK, q, 0.0)], axis=0))


def _value_parts(v, precise):
    parts = _parts(v, precise)
    lane = lax.broadcasted_iota(jnp.int32, v.shape, 1)
    ones = jnp.where(lane == 0, 1.0, 0.0).astype(BF16)
    tails = (ones,) + (jnp.zeros(v.shape, BF16),) * (len(parts) - 1)
    return tuple(jnp.concatenate([p, t], axis=1) for p, t in zip(parts, tails))


def _softmax_update(score_blocks, value_blocks, m_ref, acc_ref, rows, precise=False):
    m_prev = m_ref[rows]
    m_new = m_prev
    for s in score_blocks:
        m_new = jnp.maximum(m_new, jnp.max(s, axis=-1, keepdims=True))
    pv = None
    for s, va in zip(score_blocks, value_blocks):
        t = _dot(_parts(jnp.exp2(s - m_new), precise), va)
        pv = t if pv is None else pv + t
    acc_ref[rows] = jnp.exp2(m_prev - m_new) * acc_ref[rows] + pv
    m_ref[rows] = m_new


def _diff_finalize(lamv, g, acc, n, lam_init):
    lam = (jnp.exp(jnp.sum(lamv[0:1] * lamv[1:2], keepdims=True))
           - jnp.exp(jnp.sum(lamv[2:3] * lamv[3:4], keepdims=True)) + lam_init)
    o1 = acc[0:n, 0:DVD] / acc[0:n, DVD:DVD + 1]
    o2 = acc[n:2 * n, 0:DVD] / acc[n:2 * n, DVD:DVD + 1]
    o = o1 - lam * o2
    ms = jnp.mean(o * o, axis=-1, keepdims=True)
    return o * lax.rsqrt(ms + EPS) * g * (1.0 - lam_init)


def _attn_init(q, qs, m_s, acc, n):
    _stack_queries(q, qs, n)
    m_s[...] = jnp.full(m_s.shape, NEG, F32)
    acc[...] = jnp.zeros(acc.shape, F32)


def _attn_prompt_kernel(it, jt, q_ref, k_ref, v_ref, lamv, gdn, o_ref, qs, m_s, acc, *, tq, rc, lam_init):
    t = pl.program_id(2)
    i = it[t]
    j = jt[t]

    @pl.when(j == 0)
    def _():
        _attn_init(q_ref[...], qs, m_s, acc, tq)

    def update(masked):
        kb = k_ref[...].astype(BF16)
        va = _value_parts(v_ref[...], False)
        for c in range(2 * tq // rc):
            rows = slice(c * rc, (c + 1) * rc)
            s = _mm_nt(qs[0, rows], kb)
            if masked:
                row = lax.broadcasted_iota(jnp.int32, s.shape, 0) + (c * rc) % tq
                col = lax.broadcasted_iota(jnp.int32, s.shape, 1)
                s = jnp.where(col <= row, s, NEG)
            _softmax_update([s], [va], m_s, acc, rows)

    @pl.when(j < i)
    def _():
        update(False)

    @pl.when(j == i)
    def _():
        update(True)
        o_ref[...] = _diff_finalize(lamv[...], gdn[...], acc, tq, lam_init).astype(o_ref.dtype)


def _attn_prompt(z, k, v, lamv, g_dn_l, n_batch, seq, lam_init):
    T = z.shape[0]
    tq = 512
    rc = 256
    nq = seq // tq
    cb = 1024 // DVD
    pairs = [(i, j) for i in range(nq) for j in range(i + 1)]
    it = jnp.asarray([p[0] for p in pairs], jnp.int32)
    jt = jnp.asarray([p[1] for p in pairs], jnp.int32)
    vmem = 2 * 3 * tq * DVD * 4 + 2 * tq * DVD * 2 + 2 * tq * DVD * 2 + 2 * tq * 2 * DVD * 4 \
        + 2 * tq * LANES * 4 + 8 * rc * tq * 4 + (8 << 20)
    return pl.pallas_call(
        functools.partial(_attn_prompt_kernel, tq=tq, rc=rc, lam_init=lam_init),
        grid_spec=pltpu.PrefetchScalarGridSpec(
            num_scalar_prefetch=2,
            grid=(n_batch, H_D, len(pairs)),
            in_specs=[
                pl.BlockSpec((tq, DVD), lambda b, h, t, it, jt: (b * nq + it[t], COL_DQ * cb + h)),
                pl.BlockSpec((tq, DVD), lambda b, h, t, it, jt: (b * nq + jt[t], h)),
                pl.BlockSpec((tq, DVD), lambda b, h, t, it, jt: (b * nq + jt[t], h)),
                pl.BlockSpec((4, DQK), lambda b, h, t, it, jt: (0, 0)),
                pl.BlockSpec((None, 1, DVD), lambda b, h, t, it, jt: (h, 0, 0)),
            ],
            out_specs=pl.BlockSpec((tq, DVD), lambda b, h, t, it, jt: (b * nq + it[t], h)),
            scratch_shapes=[pltpu.VMEM((1, 2 * tq, DVD), BF16), pltpu.VMEM((2 * tq, 1), F32),
                            pltpu.VMEM((2 * tq, 2 * DVD), F32)],
        ),
        out_shape=jax.ShapeDtypeStruct((T, W_D), BF16),
        compiler_params=_cparams(3, vmem),
        name="attn_prompt",
    )(it, jt, z, k, v, lamv, g_dn_l[:, None, :])


PAGES_PER_STEP = 8


def _attn_sample_kernel(pt_ref, q_ref, *rest, n_steps, lam_init):
    del pt_ref
    kp = rest[:PAGES_PER_STEP]
    vp = rest[PAGES_PER_STEP:2 * PAGES_PER_STEP]
    kn, vn, bias, bias_new, lamv, gdn, o_ref, qs, m_s, acc = rest[2 * PAGES_PER_STEP:]
    n = q_ref.shape[0]
    step = pl.program_id(1)
    every_row = slice(0, 2 * n)
    pr = qs.shape[0] == 2

    @pl.when(step == 0)
    def _():
        _attn_init(q_ref[...], qs, m_s, acc, n)

    rows = PAGE_SIZE * H_D
    q = _load_parts(qs)
    scores = [_dot(q, _parts(kp[p][...].reshape(rows, DVD), pr), _mm_nt) + bias[...] for p in range(PAGES_PER_STEP)]
    values = [_value_parts(vp[p][...].reshape(rows, DVD), pr) for p in range(PAGES_PER_STEP)]
    _softmax_update(scores, values, m_s, acc, every_row, pr)

    @pl.when(step == n_steps - 1)
    def _():
        s = _dot(q, _parts(kn[...], pr), _mm_nt) + bias_new[...]
        _softmax_update([s], [_value_parts(vn[...], pr)], m_s, acc, every_row, pr)
        o_ref[...] = _diff_finalize(lamv[...], gdn[...], acc, n, lam_init).astype(o_ref.dtype)


def _attn_sample(zs, ks, vs, cache_k, cache_v, page_table, l, lamv, g_dn_l, n_batch, n_tok, lam_init, precise=True):
    n_pages = page_table.shape[1]
    n_steps = n_pages // PAGES_PER_STEP
    n = n_tok * H_D
    rows = PAGE_SIZE * H_D
    q = zs[:, COL_DQ * 1024:(COL_DQ + 1) * 1024].reshape(n_batch, n, DVD)
    pad = lambda t: jnp.pad(t.reshape(n_batch, n, DVD), ((0, 0), (0, LANES - n), (0, 0)))
    kn = pad(ks)
    vn = pad(vs)
    r = jnp.arange(2 * n)
    c = jnp.arange(rows)
    bias = jnp.where((r % H_D)[:, None] == (c % H_D)[None, :], 0.0, NEG).astype(F32)
    cn = jnp.arange(LANES)
    ok = ((r % H_D)[:, None] == (cn % H_D)[None, :]) & ((cn // H_D)[None, :] <= ((r % n) // H_D)[:, None]) \
        & (cn < n)[None, :]
    bias_new = jnp.where(ok, 0.0, NEG).astype(F32)
    gdn = jnp.tile(g_dn_l, (n_tok, 1))

    def page_spec(p):
        return pl.BlockSpec((None, None, PAGE_SIZE, H_D, DVD),
                            lambda b, s, pt: (l, pt[b * n_pages + s * PAGES_PER_STEP + p], 0, 0, 0))

    const = lambda shp: pl.BlockSpec(shp, lambda b, s, pt: (0,) * len(shp))
    per_b = lambda shp: pl.BlockSpec((None,) + shp, lambda b, s, pt: (b, 0, 0))
    page_bytes = rows * DVD * 4
    n_p = _n_parts(precise)
    vmem = 2 * 2 * PAGES_PER_STEP * page_bytes + 2 * (2 * n) * rows * 4 \
        + (1 + n_p) * PAGES_PER_STEP * (2 * n) * rows * 4 + n_p * PAGES_PER_STEP * rows * 3 * DVD * 2 + (8 << 20)
    out = pl.pallas_call(
        functools.partial(_attn_sample_kernel, n_steps=n_steps, lam_init=lam_init),
        grid_spec=pltpu.PrefetchScalarGridSpec(
            num_scalar_prefetch=1,
            grid=(n_batch, n_steps),
            in_specs=[per_b((n, DVD))]
            + [page_spec(p) for p in range(PAGES_PER_STEP)]
            + [page_spec(p) for p in range(PAGES_PER_STEP)]
            + [per_b((LANES, DVD)), per_b((LANES, DVD)), const((2 * n, rows)), const((2 * n, LANES)),
               const((4, DQK)), const((n, DVD))],
            out_specs=per_b((n, DVD)),
            scratch_shapes=[pltpu.VMEM((n_p, 2 * n, DVD), BF16), pltpu.VMEM((2 * n, 1), F32),
                            pltpu.VMEM((2 * n, 2 * DVD), F32)],
        ),
        out_shape=jax.ShapeDtypeStruct((n_batch, n, DVD), F32 if precise else BF16),
        compiler_params=_cparams(2, vmem),
        name="attn_sample",
    )(page_table.reshape(-1), q, *([cache_k] * PAGES_PER_STEP), *([cache_v] * PAGES_PER_STEP),
      kn, vn, bias, bias_new, lamv, gdn)
    return out.reshape(n_batch * n_tok, W_D)


def _mix_kernel(yr, yd, ga, gb, wpa, wpb, o_ref, *, pr):
    a = _dot(_parts(yr[...], pr), _parts(wpa[...], pr))
    d = _dot(_parts(yd[...], pr), _parts(wpb[...], pr))
    o_ref[...] = (jax.nn.sigmoid(ga[...]) * a + jax.nn.sigmoid(gb[...]) * d).astype(o_ref.dtype)


def _mix(yr, yd, z, w_pa, w_pb, l, tm, precise=False):
    T = yr.shape[0]
    tn = 1024
    vmem = 2 * 2 * tm * W_R * 4 + 2 * 2 * tm * tn * 4 + 2 * 2 * W_R * tn * 4 + 4 * W_R * tn * 2 \
        + 2 * tm * tn * 4 + 4 * tm * tn * 4 + (6 << 20)
    return pl.pallas_call(
        functools.partial(_mix_kernel, pr=precise),
        grid=(T // tm, D_MODEL // tn),
        in_specs=[
            pl.BlockSpec((tm, W_R), lambda m, n: (m, 0)),
            pl.BlockSpec((tm, W_D), lambda m, n: (m, 0)),
            pl.BlockSpec((tm, tn), lambda m, n: (m, COL_GA + n)),
            pl.BlockSpec((tm, tn), lambda m, n: (m, COL_GB + n)),
            pl.BlockSpec((None, W_R, tn), lambda m, n: (l, 0, n)),
            pl.BlockSpec((None, W_D, tn), lambda m, n: (l, 0, n)),
        ],
        out_specs=pl.BlockSpec((tm, tn), lambda m, n: (m, n)),
        out_shape=jax.ShapeDtypeStruct((T, D_MODEL), F32 if precise else BF16),
        compiler_params=_cparams(2, vmem),
        name="mix",
    )(yr, yd, z, z, w_pa, w_pb)


def _outproj_kernel(mix, wo, x_ref, gt, g2, sc, sh, x1_ref, h2_ref):
    pr = wo.dtype == F32
    x1 = x_ref[...] + gt[...] * _dot(_parts(mix[...], pr), _parts(wo[...], pr))
    x1_ref[...] = x1
    ms = jnp.mean(x1 * x1, axis=-1, keepdims=True)
    h = x1 * lax.rsqrt(ms + EPS) * g2[...]
    h = (h * (1.0 + sc[...]) + sh[...]).astype(h2_ref.dtype)
    if len(h2_ref.shape) == 3:
        _store_chunked(h2_ref, h)
    else:
        h2_ref[...] = h


def _outproj(mix, wo, l, x, mods, g2, tm, chunked_h2):
    T = x.shape[0]
    precise = wo.dtype == F32
    row = lambda: pl.BlockSpec((tm, D_MODEL), lambda m: (m, 0))
    if chunked_h2:
        h2_spec = pl.BlockSpec((tm, ROW_CHUNKS, LANES), lambda m: (m, 0, 0))
        h2_shape = jax.ShapeDtypeStruct((T, ROW_CHUNKS, LANES), F32)
    else:
        h2_spec, h2_shape = row(), jax.ShapeDtypeStruct((T, D_MODEL), F32 if precise else BF16)
    vmem = 2 * tm * D_MODEL * (4 + 4 + 4 + 4) + 2 * D_MODEL * D_MODEL * wo.dtype.itemsize \
        + (2 * D_MODEL * D_MODEL * 2 if precise else 0) + 4 * tm * D_MODEL * 4 + (6 << 20)
    return pl.pallas_call(
        _outproj_kernel,
        grid=(T // tm,),
        in_specs=[row(), pl.BlockSpec((None, D_MODEL, D_MODEL), lambda m: (l, 0, 0)), row(),
                  mods.spec(2, tm), pl.BlockSpec((1, D_MODEL), lambda m: (0, 0)),
                  mods.spec(4, tm), mods.spec(3, tm)],
        out_specs=[row(), h2_spec],
        out_shape=[jax.ShapeDtypeStruct((T, D_MODEL), F32), h2_shape],
        compiler_params=_cparams(1, vmem),
        name="outproj",
    )(mix, wo, x, mods.arr, g2, mods.arr, mods.arr)


def _ffn_up_kernel(te, first, nv, xs, w1, w3, o_ref, w1b, w3b):
    del te
    m = pl.program_id(1)

    @pl.when(first[m] == 1)
    def _():
        _store_parts(w1b, w1[...])
        _store_parts(w3b, w3[...])

    @pl.when(m < nv[0])
    def _():
        x = _parts(xs[...], w1b.shape[0] == 2)
        o_ref[...] = (_silu(_dot(x, _load_parts(w1b))) * _dot(x, _load_parts(w3b))).astype(o_ref.dtype)

    @pl.when(m >= nv[0])
    def _():
        o_ref[...] = jnp.zeros(o_ref.shape, o_ref.dtype)


def _ffn_up(xs, w1, w3, te, first, nv, tm, tf, precise=False):
    P = xs.shape[0]
    ff = w1.shape[-1]
    n_mt = pl.cdiv(P, tm)
    xb = xs.dtype.itemsize
    n_p = _n_parts(precise)
    vmem = 2 * tm * D_MODEL * xb + 2 * 2 * D_MODEL * tf * 4 + 2 * n_p * D_MODEL * tf * 2 + 2 * tm * tf * 4 \
        + 4 * tm * tf * 4 + n_p * tm * D_MODEL * 2 + (6 << 20)
    return pl.pallas_call(
        _ffn_up_kernel,
        grid_spec=pltpu.PrefetchScalarGridSpec(
            num_scalar_prefetch=3,
            grid=(pl.cdiv(ff, tf), n_mt),
            in_specs=[
                pl.BlockSpec((tm, D_MODEL), lambda f, m, te, fi, nv: (m, 0)),
                pl.BlockSpec((None, D_MODEL, tf), lambda f, m, te, fi, nv: (te[m], 0, f)),
                pl.BlockSpec((None, D_MODEL, tf), lambda f, m, te, fi, nv: (te[m], 0, f)),
            ],
            out_specs=pl.BlockSpec((tm, tf), lambda f, m, te, fi, nv: (m, f)),
            scratch_shapes=[pltpu.VMEM((n_p, D_MODEL, tf), BF16), pltpu.VMEM((n_p, D_MODEL, tf), BF16)],
        ),
        out_shape=jax.ShapeDtypeStruct((P, ff), F32 if precise else BF16),
        compiler_params=_cparams(2, vmem),
        name="ffn_up",
    )(te, first, nv, xs, w1, w3)


def _ffn_down_kernel(te, first, nv, act, w2, o_ref, w2b):
    del te
    m = pl.program_id(1)

    @pl.when(first[m] == 1)
    def _():
        _store_parts(w2b, w2[...])

    @pl.when(m < nv[0])
    def _():
        o_ref[...] = _dot(_parts(act[...], w2b.shape[0] == 2), _load_parts(w2b))

    @pl.when(m >= nv[0])
    def _():
        o_ref[...] = jnp.zeros(o_ref.shape, o_ref.dtype)


def _ffn_down(act, w2, te, first, nv, tm, tn, precise=False):
    P, ff = act.shape
    n_mt = pl.cdiv(P, tm)
    n_p = _n_parts(precise)
    vmem = (2 + n_p) * tm * ff * act.dtype.itemsize + 2 * ff * tn * 4 + n_p * ff * tn * 2 + 4 * tm * tn * 4 + (4 << 20)
    return pl.pallas_call(
        _ffn_down_kernel,
        grid_spec=pltpu.PrefetchScalarGridSpec(
            num_scalar_prefetch=3,
            grid=(D_MODEL // tn, n_mt),
            in_specs=[
                pl.BlockSpec((tm, ff), lambda n, m, te, fi, nv: (m, 0)),
                pl.BlockSpec((None, ff, tn), lambda n, m, te, fi, nv: (te[m], 0, n)),
            ],
            out_specs=pl.BlockSpec((tm, tn), lambda n, m, te, fi, nv: (m, n)),
            scratch_shapes=[pltpu.VMEM((n_p, ff, tn), BF16)],
        ),
        out_shape=jax.ShapeDtypeStruct((P, D_MODEL), F32),
        compiler_params=_cparams(2, vmem),
        name="ffn_down",
    )(te, first, nv, act, w2)


def _single_expert_tiles(P, tm, e):
    n_mt = pl.cdiv(P, tm)
    te = jnp.full((n_mt,), e, jnp.int32)
    first = jnp.zeros((n_mt,), jnp.int32).at[0].set(1)
    return te, first, jnp.full((1,), n_mt, jnp.int32)


def _residual_kernel(x1, gt, y, o_ref):
    o_ref[...] = x1[...] + gt[...] * y[...]


def _residual(x1, mods, y, tm):
    T = x1.shape[0]
    row = lambda: pl.BlockSpec((tm, D_MODEL), lambda m: (m, 0))
    return pl.pallas_call(
        _residual_kernel,
        grid=(T // tm,),
        in_specs=[row(), mods.spec(5, tm), row()],
        out_specs=row(),
        out_shape=jax.ShapeDtypeStruct((T, D_MODEL), F32),
        compiler_params=_cparams(1, 8 * tm * D_MODEL * 4 + (4 << 20)),
        name="residual",
    )(x1, mods.arr, y)


def _router_kernel(h_ref, wr_ref, o_ref):
    logits = None
    tm = h_ref.shape[0] // ROW_CHUNKS
    for c in range(ROW_CHUNKS):
        h_c = h_ref[pl.ds(c, tm, stride=ROW_CHUNKS), :]
        part = jnp.dot(h_c, wr_ref[c * LANES:(c + 1) * LANES, :], preferred_element_type=F32,
                       precision=lax.Precision.HIGHEST)
        logits = part if logits is None else logits + part
    lane = lax.broadcasted_iota(jnp.int32, logits.shape, 1)
    logits = jnp.where(lane < N_EXPERTS, logits, NEG)
    v1 = jnp.max(logits, axis=-1, keepdims=True)
    i1 = jnp.min(jnp.where(logits == v1, lane, LANES), axis=-1, keepdims=True)
    rest = jnp.where(lane == i1, NEG, logits)
    v2 = jnp.max(rest, axis=-1, keepdims=True)
    i2 = jnp.min(jnp.where(rest == v2, lane, LANES), axis=-1, keepdims=True)
    e2 = jnp.exp(v2 - v1)
    w1 = 1.0 / (1.0 + e2)
    w2 = e2 / (1.0 + e2)
    o_ref[...] = jnp.where(lane == 0, i1.astype(F32),
                           jnp.where(lane == 1, i2.astype(F32),
                                     jnp.where(lane == 2, w1, jnp.where(lane == 3, w2, 0.0))))


def _router(h2, w_router_l, tm):
    T = h2.shape[0]
    wr = jnp.pad(w_router_l, ((0, 0), (0, LANES - N_EXPERTS)))
    return pl.pallas_call(
        _router_kernel,
        grid=(pl.cdiv(T, tm),),
        in_specs=[pl.BlockSpec((tm * ROW_CHUNKS, LANES), lambda m: (m, 0)),
                  pl.BlockSpec((D_MODEL, LANES), lambda m: (0, 0))],
        out_specs=pl.BlockSpec((tm, LANES), lambda m: (m, 0)),
        out_shape=jax.ShapeDtypeStruct((T, LANES), F32),
        compiler_params=_cparams(1, 6 * tm * D_MODEL * 4 + (8 << 20)),
        name="router",
    )(h2.reshape(T * ROW_CHUNKS, LANES), wr)


DMA_UNROLL = 8


def _row_copy(src_hbm, row, dst_vmem, slot, sem):
    return pltpu.make_async_copy(src_hbm.at[pl.ds(row, 1)], dst_vmem.at[pl.ds(slot, 1)], sem)


def _chunked_row_copy(src_hbm, row, dst_vmem, slot, sem):
    return pltpu.make_async_copy(src_hbm.at[row], dst_vmem.at[pl.ds(slot * ROW_CHUNKS, ROW_CHUNKS)], sem)


def _dispatch_kernel(src, h_hbm, o_ref, buf, sem, *, tm):
    base = pl.program_id(0) * tm

    def start(r, carry):
        _chunked_row_copy(h_hbm, src[base + r], buf, r, sem).start()
        return carry

    def wait(r, carry):
        _chunked_row_copy(h_hbm, 0, buf, r, sem).wait()
        return carry

    lax.fori_loop(0, tm, start, 0, unroll=DMA_UNROLL)
    lax.fori_loop(0, tm, wait, 0, unroll=DMA_UNROLL)
    for c in range(ROW_CHUNKS):
        o_ref[:, c * LANES:(c + 1) * LANES] = buf[pl.ds(c, tm, stride=ROW_CHUNKS), :].astype(o_ref.dtype)


def _dispatch(h2, src, tm, out_dtype):
    P = src.shape[0]
    return pl.pallas_call(
        functools.partial(_dispatch_kernel, tm=tm),
        grid_spec=pltpu.PrefetchScalarGridSpec(
            num_scalar_prefetch=1,
            grid=(P // tm,),
            in_specs=[pl.BlockSpec(memory_space=pl.ANY)],
            out_specs=pl.BlockSpec((tm, D_MODEL), lambda m, src: (m, 0)),
            scratch_shapes=[pltpu.VMEM((tm * ROW_CHUNKS, LANES), F32), pltpu.SemaphoreType.DMA(())],
        ),
        out_shape=jax.ShapeDtypeStruct((P, D_MODEL), out_dtype),
        compiler_params=_cparams(1, 2 * tm * D_MODEL * 4 + 3 * tm * D_MODEL * 4 + (4 << 20)),
        name="dispatch",
    )(src, h2)


def _combine_kernel(pos, y_hbm, rout, x1, gt, o_ref, buf, sem, *, tm):
    base = pl.program_id(0) * tm

    def start(r, carry):
        for s in range(TOP_K):
            _row_copy(y_hbm, pos[(base + r) * TOP_K + s], buf.at[s], r, sem).start()
        return carry

    def wait(r, carry):
        for s in range(TOP_K):
            _row_copy(y_hbm, 0, buf.at[s], r, sem).wait()
        return carry

    lax.fori_loop(0, tm, start, 0, unroll=DMA_UNROLL)
    lax.fori_loop(0, tm, wait, 0, unroll=DMA_UNROLL)
    g = rout[...]
    ffn = g[:, 2:3] * buf[0] + g[:, 3:4] * buf[1]
    o_ref[...] = x1[...] + gt[...] * ffn


def _combine(y, pos, rout, x1, mods, tm):
    T = x1.shape[0]
    row = lambda: pl.BlockSpec((tm, D_MODEL), lambda m, pos: (m, 0))
    return pl.pallas_call(
        functools.partial(_combine_kernel, tm=tm),
        grid_spec=pltpu.PrefetchScalarGridSpec(
            num_scalar_prefetch=1,
            grid=(T // tm,),
            in_specs=[pl.BlockSpec(memory_space=pl.ANY),
                      pl.BlockSpec((tm, LANES), lambda m, pos: (m, 0)),
                      row(), mods.spec(5, tm)],
            out_specs=row(),
            scratch_shapes=[pltpu.VMEM((TOP_K, tm, D_MODEL), F32), pltpu.SemaphoreType.DMA(())],
        ),
        out_shape=jax.ShapeDtypeStruct((T, D_MODEL), F32),
        compiler_params=_cparams(1, (TOP_K + 8) * tm * D_MODEL * 4 + (4 << 20)),
        name="combine",
    )(pos, y, rout, x1, mods.arr)


def _route_tables(rout, tm, n_mt):
    n_tok = rout.shape[0]
    ef = rout[:, :TOP_K].astype(jnp.int32).reshape(-1)
    onehot = (ef[:, None] == jnp.arange(N_EXPERTS)[None, :]).astype(jnp.int32)
    counts = jnp.sum(onehot, axis=0)
    padded = ((counts + tm - 1) // tm) * tm
    ends = jnp.cumsum(padded)
    offs = ends - padded
    rank = jnp.sum((jnp.cumsum(onehot, axis=0) - onehot) * onehot, axis=1)
    pos = (offs[ef] + rank).astype(jnp.int32)
    src = jnp.zeros((n_mt * tm,), jnp.int32).at[pos].set(jnp.arange(n_tok * TOP_K, dtype=jnp.int32) // TOP_K)
    n_valid = (ends[-1] // tm).astype(jnp.int32)
    starts = jnp.arange(n_mt, dtype=jnp.int32) * tm
    te = jnp.minimum(jnp.searchsorted(ends, starts, side="right"), N_EXPERTS - 1).astype(jnp.int32)
    te = jnp.where(jnp.arange(n_mt) < n_valid, te, te[jnp.maximum(n_valid - 1, 0)])
    first = jnp.concatenate([jnp.ones((1,), jnp.int32), (te[1:] != te[:-1]).astype(jnp.int32)])
    return pos, src, te, first, n_valid.reshape(1)


def kernel(x_prompt, x_sample, cache_k, cache_v, state_ret, page_table, c_prompt, c_sample, w_ada, b_ada,
           g_norm1, g_norm2, w_in, g_qn, g_kn, lam_q1, lam_k1, lam_q2, lam_k2, g_ret, g_dn, w_pa, w_pb, w_o,
           w_ff1, w_ff3, w_ff2, w_router, w_e1, w_e3, w_e2):
    B, S, _ = x_prompt.shape
    Bd, Sd, _ = x_sample.shape
    n_pages = page_table.shape[1]
    past = n_pages * cache_k.shape[2]
    Tp, Ts = B * S, Bd * Sd
    tm_p = 512

    c_all = jnp.concatenate([c_prompt, c_sample], axis=0)
    c_all = jnp.pad(c_all, ((0, (-c_all.shape[0]) % 8), (0, 0)))
    mod = _adaln(c_all, w_ada, b_ada).reshape(DEPTH, c_all.shape[0], 6, D_MODEL)

    pos_p = jnp.arange(S)
    pos_s = past + jnp.arange(Sd)
    r0_p = jnp.zeros((B, H_R, DR, DR), F32)
    n_exp_w = w_e1.shape[0] * N_EXPERTS
    we1 = w_e1.reshape(n_exp_w, D_MODEL, -1)
    we3 = w_e3.reshape(n_exp_w, D_MODEL, -1)
    we2 = w_e2.reshape(n_exp_w, -1, D_MODEL)

    xp = x_prompt.reshape(Tp, D_MODEL)
    xs = x_sample.reshape(Ts, D_MODEL)
    kp, vp, rp, ksl, vsl, rsl = [], [], [], [], [], []
    for l in range(DEPTH):
        lam_init = 0.8 - 0.6 * math.exp(-0.3 * l)
        lamv = jnp.stack([lam_q1[l], lam_k1[l], lam_q2[l], lam_k2[l]]).astype(F32)
        mods_p = _Mods(mod[l, :B], "prompt", S)
        mods_s = _Mods(mod[l, B:B + Bd], "sample", Sd)
        wo_b = w_o[l].astype(BF16)[None]
        g2 = g_norm2[l][None]
        moe = l % 2 == 1

        zp, k_p, v_p = _inproj(xp, mods_p, l, g_norm1, w_in, g_qn, g_kn, 1024)
        yr_p, r_new_p = _retention(zp, pos_p, r0_p, g_ret[l], B, RET_CHUNK, BF16)
        yd_p = _attn_prompt(zp, k_p, v_p, lamv, g_dn[l], B, S, lam_init)
        mix_p = _mix(yr_p, yd_p, zp, w_pa, w_pb, l, tm_p)
        x1_p, h2_p = _outproj(mix_p, wo_b, 0, xp, mods_p, g2, tm_p, moe)
        kp.append(k_p.reshape(B, S, H_D, 2 * DQK))
        vp.append(v_p.reshape(B, S, H_D, DVD))
        rp.append(r_new_p)

        zs, k_s, v_s = _inproj(xs, mods_s, l, g_norm1, w_in, g_qn, g_kn, Ts, precise=True)
        yr_s, r_new_s = _retention(zs, pos_s, state_ret[l], g_ret[l], Bd, Sd, F32, precise=True)
        yd_s = _attn_sample(zs, k_s, v_s, cache_k, cache_v, page_table, l, lamv, g_dn[l], Bd, Sd, lam_init)
        mix_s = _mix(yr_s, yd_s, zs, w_pa, w_pb, l, Ts, precise=True)
        x1_s, h2_s = _outproj(mix_s, w_o, l, xs, mods_s, g2, Ts, moe)
        ksl.append(k_s.reshape(Bd, Sd, H_D, 2 * DQK))
        vsl.append(v_s.reshape(Bd, Sd, H_D, DVD))
        rsl.append(r_new_s)

        i = l // 2
        if not moe:
            outs = []
            for h2, x1, mods, tm, pr in ((h2_p, x1_p, mods_p, tm_p, False), (h2_s, x1_s, mods_s, Ts, True)):
                te, first, nv = _single_expert_tiles(h2.shape[0], tm, i)
                act = _ffn_up(h2, w_ff1, w_ff3, te, first, nv, tm, 512 if pr else 1024, pr)
                y = _ffn_down(act, w_ff2, te, first, nv, tm, 512, pr)
                outs.append(_residual(x1, mods, y, tm))
            xp, xs = outs
        else:
            def routed(h2, tm_e, pr):
                n_tok = h2.shape[0]
                n_mt = (n_tok * TOP_K + N_EXPERTS * (tm_e - 1)) // tm_e + 1
                rout = _router(h2, w_router[i], min(n_tok, 512))
                pos, src, te, first, nv = _route_tables(rout, tm_e, n_mt)
                te = te + i * N_EXPERTS
                xs_sorted = _dispatch(h2, src, tm_e, F32 if pr else BF16)
                act = _ffn_up(xs_sorted, we1, we3, te, first, nv, tm_e, 512 if pr else 1024, pr)
                return _ffn_down(act, we2, te, first, nv, tm_e, 512, pr), pos, rout

            if any(later % 2 == 1 for later in range(l + 1, DEPTH)):
                y, pos, rout = routed(h2_p, 512, False)
                xp = _combine(y, pos, rout, x1_p, mods_p, 256)
                y, pos, rout = routed(h2_s, Ts, True)
                xs = _combine(y, pos, rout, x1_s, mods_s, Ts)
            else:
                y, pos, rout = routed(jnp.concatenate([h2_p, h2_s], axis=0), 512, False)
                xp = _combine(y, pos[:Tp * TOP_K], rout[:Tp], x1_p, mods_p, 256)
                xs = _combine(y, pos[Tp * TOP_K:], rout[Tp:], x1_s, mods_s, Ts)

    return (xp.reshape(B, S, D_MODEL), xs.reshape(Bd, Sd, D_MODEL),
            jnp.stack(kp), jnp.stack(vp), jnp.stack(rp), jnp.stack(ksl), jnp.stack(vsl), jnp.stack(rsl))
```
